```python
import jax, jax.numpy as jnp
from jax import lax
import numpy as np

D_MODEL = 1024
BATCH = 16
SEQ = 2048
DEPTH = 4

N_MIXERS = 4
GW = D_MODEL // N_MIXERS
HEAD_DIM = 64
SG_HEADS = GW // HEAD_DIM
SG_CHUNK = 128
CONV_WIDTH = 31
NSA_HEADS = GW // HEAD_DIM
CMP_BLOCK = 32
CMP_STRIDE = 16
SLC_BLOCK = 64
SLC_TOPK = 8
WIN = 512
Q_BLOCK = 128
FORCE_BONUS = 1e4
NEG = -1e30
POOL_WINDOWS = (2, 4, 8, 16)
POOL_GROUP = GW // len(POOL_WINDOWS)
FFN_HIDDEN = -(-8 * D_MODEL // (3 * 256)) * 256
RMS_EPS = 1e-6
LN_EPS = 1e-5
IN_SIZES = (GW, GW, GW, GW, GW, 6 * HEAD_DIM, 3 * NSA_HEADS, GW)
IN_COLS = sum(IN_SIZES)

kernel_name = 'hybrid_parallel_sgmlp_conformer_nsa_pool'


def rmsnorm(x, g):
    xf = x.astype(jnp.float32)
    r = lax.rsqrt(jnp.mean(xf * xf, axis=-1, keepdims=True) + RMS_EPS)
    return (xf * r).astype(x.dtype) * g


def layernorm(x, g, b):
    xf = x.astype(jnp.float32)
    mu = jnp.mean(xf, axis=-1, keepdims=True)
    var = jnp.mean(jnp.square(xf - mu), axis=-1, keepdims=True)
    return ((xf - mu) * lax.rsqrt(var + LN_EPS)).astype(x.dtype) * g + b


def spatial_gating(u, v, ln_g, w_s, b_s):
    B, S, _ = u.shape
    nc = S // SG_CHUNK
    v = layernorm(v, ln_g, jnp.zeros((), v.dtype))
    v = v.reshape(B, nc, SG_CHUNK, SG_HEADS, HEAD_DIM)
    w = jnp.tril(w_s)
    sv = jnp.einsum('hts,bcshd->bcthd', w, v) + b_s.T[None, None, :, :, None]
    return u * sv.reshape(B, S, GW)


def conformer_conv(a, gate, w_dw, b_dw, ln_g, ln_b, w_pw, b_pw):
    h = a * jax.nn.sigmoid(gate)
    h = lax.conv_general_dilated(
        h, w_dw[:, None, :], window_strides=(1,), padding=[(CONV_WIDTH - 1, 0)],
        dimension_numbers=('NWC', 'WIO', 'NWC'), feature_group_count=GW) + b_dw
    h = jax.nn.silu(layernorm(h, ln_g, ln_b))
    return h @ w_pw + b_pw


def nsa_mixer(q, kv, gate_logits, pos_k, pos_v, w1k, w2k, w1v, w2v):
    B, S, _ = q.shape
    H, D = NSA_HEADS, HEAD_DIM
    dt = q.dtype
    q = q.reshape(B, S, H, D) * (D ** -0.5)
    k_c, v_c, k_s, v_s, k_w, v_w = jnp.split(kv, 6, axis=-1)
    pos = np.arange(S)

    n_cmp = (S - CMP_BLOCK) // CMP_STRIDE + 1
    blk_idx = np.arange(n_cmp)[:, None] * CMP_STRIDE + np.arange(CMP_BLOCK)[None, :]

    def compress(t, pe, w1, w2):
        tb = (t[:, blk_idx] + pe).reshape(B, n_cmp, CMP_BLOCK * D)
        return jax.nn.silu(tb @ w1) @ w2

    kc = compress(k_c, pos_k, w1k, w2k)
    vc = compress(v_c, pos_v, w1v, w2v)
    cmp_end = np.arange(n_cmp) * CMP_STRIDE + CMP_BLOCK - 1
    cmask = cmp_end[None, :] <= pos[:, None]
    has_cmp = cmask.any(axis=-1)[:, None]
    s = jnp.einsum('bshd,bcd->bhsc', q, kc).astype(jnp.float32)
    p_cmp = jnp.where(has_cmp, jax.nn.softmax(jnp.where(cmask, s, NEG), axis=-1), 0.0)
    o_cmp = jnp.einsum('bhsc,bcd->bshd', p_cmp.astype(dt), vc)

    n_slc = S // SLC_BLOCK
    c_start = np.arange(n_cmp) * CMP_STRIDE
    j_start = np.arange(n_slc) * SLC_BLOCK
    overlap = ((c_start[:, None] <= j_start[None, :] + SLC_BLOCK - 1)
               & (c_start[:, None] + CMP_BLOCK - 1 >= j_start[None, :])).astype(np.float32)
    imp = jnp.einsum('bhsc,cj->bsj', p_cmp, overlap)
    cur = pos // SLC_BLOCK
    jj = np.arange(n_slc)[None, :]
    valid_blk = j_start[None, :] <= pos[:, None]
    forced = (jj == 0) | (jj == cur[:, None]) | (jj == cur[:, None] - 1)
    score = jnp.where(valid_blk, imp + jnp.where(forced, FORCE_BONUS, 0.0), NEG)
    k_sel = min(SLC_TOPK, n_slc)
    top_val, top_idx = lax.top_k(score, k_sel)
    sel_ok = top_val > NEG / 2

    nb = S // Q_BLOCK
    qb = q.reshape(B, nb, Q_BLOCK, H, D).transpose(1, 0, 2, 3, 4)
    ib = top_idx.reshape(B, nb, Q_BLOCK, k_sel).transpose(1, 0, 2, 3)
    okb = sel_ok.reshape(B, nb, Q_BLOCK, k_sel).transpose(1, 0, 2, 3)
    qposb = jnp.arange(S).reshape(nb, Q_BLOCK)

    def sel_block(args):
        qc, ic, okc, qp = args
        tok = ic[..., None] * SLC_BLOCK + jnp.arange(SLC_BLOCK)
        kg = jax.vmap(lambda t, i: t[i])(k_s, tok)
        vg = jax.vmap(lambda t, i: t[i])(v_s, tok)
        m = okc[..., None] & (tok <= qp[None, :, None, None])
        sc = jnp.einsum('bqhd,bqkld->bhqkl', qc, kg).astype(jnp.float32)
        sc = jnp.where(m[:, None], sc, NEG).reshape(B, H, Q_BLOCK, k_sel * SLC_BLOCK)
        pr = jax.nn.softmax(sc, axis=-1).reshape(B, H, Q_BLOCK, k_sel, SLC_BLOCK)
        return jnp.einsum('bhqkl,bqkld->bqhd', pr.astype(dt), vg)

    o_slc = lax.map(sel_block, (qb, ib, okb, qposb))
    o_slc = o_slc.transpose(1, 0, 2, 3, 4).reshape(B, S, H, D)

    kpad = jnp.pad(k_w, ((0, 0), (WIN, 0), (0, 0)))
    vpad = jnp.pad(v_w, ((0, 0), (WIN, 0), (0, 0)))
    band = np.arange(nb)[:, None] * Q_BLOCK + np.arange(WIN + Q_BLOCK)[None, :]
    kb = kpad[:, band]
    vb = vpad[:, band]
    rel = np.arange(WIN + Q_BLOCK)[None, :] - WIN - np.arange(Q_BLOCK)[:, None]
    wmask = ((rel <= 0) & (rel > -WIN))[None] & ((band - WIN) >= 0)[:, None, :]
    qw = q.reshape(B, nb, Q_BLOCK, H, D)
    sw = jnp.einsum('bnqhd,bnkd->bhnqk', qw, kb).astype(jnp.float32)
    pw = jax.nn.softmax(jnp.where(wmask, sw, NEG), axis=-1)
    o_win = jnp.einsum('bhnqk,bnkd->bnqhd', pw.astype(dt), vb).reshape(B, S, H, D)

    g = jax.nn.sigmoid(gate_logits.reshape(B, S, H, 3))
    o = g[..., 0:1] * o_cmp + g[..., 1:2] * o_slc + g[..., 2:3] * o_win
    return o.reshape(B, S, GW)


def pool_mixer(xd, w_pool, scale):
    B, S, _ = xd.shape
    xf = xd.astype(jnp.float32)
    cs = jnp.pad(jnp.cumsum(xf, axis=1), ((0, 0), (1, 0), (0, 0)))
    t1 = np.arange(1, S + 1)
    outs = []
    for gi, w in enumerate(POOL_WINDOWS):
        sl = slice(gi * POOL_GROUP, (gi + 1) * POOL_GROUP)
        lo = np.maximum(t1 - w, 0)
        cnt = np.minimum(t1, w).astype(np.float32)
        mean = (cs[:, 1:, sl] - cs[:, lo, sl]) / cnt[None, :, None]
        outs.append((mean - xf[..., sl]).astype(xd.dtype) @ w_pool[gi])
    return jnp.concatenate(outs, axis=-1) * scale


def hybrid_layer(x, g_pre_mix, g_post_mix, g_pre_ffn, g_post_ffn, w_in,
                 sg_ln_g, sg_w, sg_b, cv_w, cv_b, cv_ln_g, cv_ln_b, cv_pw, cv_pw_b,
                 cmp_pos_k, cmp_pos_v, cmp_w1_k, cmp_w2_k, cmp_w1_v, cmp_w2_v,
                 pool_w, pool_scale, w_out, ffn_w_gu, ffn_w_down):
    h = rmsnorm(x, g_pre_mix)
    z = h @ w_in
    offs = [int(o) for o in np.cumsum(IN_SIZES)[:-1]]
    a_u, a_v, b_a, b_g, c_q, c_kv, c_gate, d_in = jnp.split(z, offs, axis=-1)
    y_a = spatial_gating(a_u, a_v, sg_ln_g, sg_w, sg_b)
    y_b = conformer_conv(b_a, b_g, cv_w, cv_b, cv_ln_g, cv_ln_b, cv_pw, cv_pw_b)
    y_c = nsa_mixer(c_q, c_kv, c_gate, cmp_pos_k, cmp_pos_v, cmp_w1_k, cmp_w2_k, cmp_w1_v, cmp_w2_v)
    y_d = pool_mixer(d_in, pool_w, pool_scale)
    mix = jnp.concatenate([y_a, y_b, y_c, y_d], axis=-1) @ w_out
    x = x + rmsnorm(mix, g_post_mix)
    h = rmsnorm(x, g_pre_ffn)
    gate, up = jnp.split(h @ ffn_w_gu, 2, axis=-1)
    f = (jax.nn.silu(gate) * up) @ ffn_w_down
    return x + rmsnorm(f, g_post_ffn)


def setup_inputs(seed: int = 0) -> dict:
    key = jax.random.key(seed)
    ks = jax.random.split(key, 32)
    f32 = jnp.float32

    def nrm(k, shape, scale):
        return jax.random.normal(k, shape, f32) * scale

    def gain(k, shape):
        return 1.0 + nrm(k, shape, 0.05)

    L = DEPTH
    return {
        'x': nrm(ks[0], (BATCH, SEQ, D_MODEL), 1.0),
        'g_pre_mix': gain(ks[1], (L, D_MODEL)),
        'g_post_mix': gain(ks[2], (L, D_MODEL)),
        'g_pre_ffn': gain(ks[3], (L, D_MODEL)),
        'g_post_ffn': gain(ks[4], (L, D_MODEL)),
        'w_in': nrm(ks[5], (L, D_MODEL, IN_COLS), D_MODEL ** -0.5),
        'sg_ln_g': gain(ks[6], (L, GW)),
        'sg_w': nrm(ks[7], (L, SG_HEADS, SG_CHUNK, SG_CHUNK), SG_CHUNK ** -0.5),
        'sg_b': 1.0 + nrm(ks[8], (L, SG_HEADS, SG_CHUNK), 0.01),
        'cv_w': nrm(ks[9], (L, CONV_WIDTH, GW), CONV_WIDTH ** -0.5),
        'cv_b': nrm(ks[10], (L, GW), 0.01),
        'cv_ln_g': gain(ks[11], (L, GW)),
        'cv_ln_b': nrm(ks[12], (L, GW), 0.01),
        'cv_pw': nrm(ks[13], (L, GW, GW), GW ** -0.5),
        'cv_pw_b': nrm(ks[14], (L, GW), 0.01),
        'cmp_pos_k': nrm(ks[15], (L, CMP_BLOCK, HEAD_DIM), 0.1),
        'cmp_pos_v': nrm(ks[16], (L, CMP_BLOCK, HEAD_DIM), 0.1),
        'cmp_w1_k': nrm(ks[17], (L, CMP_BLOCK * HEAD_DIM, HEAD_DIM), (CMP_BLOCK * HEAD_DIM) ** -0.5),
        'cmp_w2_k': nrm(ks[18], (L, HEAD_DIM, HEAD_DIM), HEAD_DIM ** -0.5),
        'cmp_w1_v': nrm(ks[19], (L, CMP_BLOCK * HEAD_DIM, HEAD_DIM), (CMP_BLOCK * HEAD_DIM) ** -0.5),
        'cmp_w2_v': nrm(ks[20], (L, HEAD_DIM, HEAD_DIM), HEAD_DIM ** -0.5),
        'pool_w': nrm(ks[21], (L, len(POOL_WINDOWS), POOL_GROUP, POOL_GROUP), POOL_GROUP ** -0.5),
        'pool_scale': gain(ks[22], (L, GW)),
        'w_out': nrm(ks[23], (L, D_MODEL, D_MODEL), D_MODEL ** -0.5),
        'ffn_w_gu': nrm(ks[24], (L, D_MODEL, 2 * FFN_HIDDEN), D_MODEL ** -0.5),
        'ffn_w_down': nrm(ks[25], (L, FFN_HIDDEN, D_MODEL), FFN_HIDDEN ** -0.5),
    }


def reference(x, g_pre_mix, g_post_mix, g_pre_ffn, g_post_ffn, w_in,
              sg_ln_g, sg_w, sg_b, cv_w, cv_b, cv_ln_g, cv_ln_b, cv_pw, cv_pw_b,
              cmp_pos_k, cmp_pos_v, cmp_w1_k, cmp_w2_k, cmp_w1_v, cmp_w2_v,
              pool_w, pool_scale, w_out, ffn_w_gu, ffn_w_down):
    for l in range(DEPTH):
        x = hybrid_layer(x, g_pre_mix[l], g_post_mix[l], g_pre_ffn[l], g_post_ffn[l], w_in[l],
                         sg_ln_g[l], sg_w[l], sg_b[l], cv_w[l], cv_b[l], cv_ln_g[l], cv_ln_b[l],
                         cv_pw[l], cv_pw_b[l], cmp_pos_k[l], cmp_pos_v[l], cmp_w1_k[l], cmp_w2_k[l],
                         cmp_w1_v[l], cmp_w2_v[l], pool_w[l], pool_scale[l], w_out[l],
                         ffn_w_gu[l], ffn_w_down[l])
    return x
```

```python
import functools

import jax
import jax.numpy as jnp
from jax import lax
from jax.experimental import pallas as pl
from jax.experimental.pallas import tpu as pltpu

F32 = jnp.float32
BF16 = jnp.bfloat16

D_MODEL = 1024
GW = 256
HEAD_DIM = 64
N_HEADS = 4
SG_CHUNK = 128
CONV_WIDTH = 31
CMP_BLOCK = 32
CMP_STRIDE = 16
SLC_BLOCK = 64
SLC_TOPK = 8
WIN = 512
Q_BLOCK = 128
FORCE_BONUS = 1e4
NEG = -1e30
POOL_WINDOWS = (2, 4, 8, 16)
FFN_HIDDEN = 2816
RMS_EPS = 1e-6
LN_EPS = 1e-5
Z_COLS = 2048
KV_OFF = 1536
N_CMP_PAD = 128
N_SLC = 32

TM = 512
TS = 512
CONV_HALO = 32
POOL_HALO = 16
SUB = 64
SLC_TK = 512
FFN_CHUNK = 1408
CMP_TB = 4
VMEM_LIMIT = 56 * 1024 * 1024


def _cparams(sem):
    return pltpu.CompilerParams(dimension_semantics=sem, vmem_limit_bytes=VMEM_LIMIT)


def _nt(a, b):
    return lax.dot_general(a, b, (((1,), (1,)), ((), ())), preferred_element_type=F32)


def _dot(a, b):
    return jnp.dot(a, b, preferred_element_type=F32)


def _inproj_kernel(x_ref, g_ref, w_ref, z_ref, kvc_ref):
    x = x_ref[...]
    r = lax.rsqrt(jnp.mean(x * x, axis=-1, keepdims=True) + RMS_EPS)
    h = ((x * r) * g_ref[...]).astype(BF16)
    z = _dot(h, w_ref[...])
    z_ref[...] = z
    kvc_ref[...] = z[:, KV_OFF:KV_OFF + 128]


def _inproj(x2, g, w):
    n = x2.shape[0]
    return pl.pallas_call(
        _inproj_kernel,
        grid=(n // TM,),
        in_specs=[
            pl.BlockSpec((TM, D_MODEL), lambda i: (i, 0)),
            pl.BlockSpec((1, D_MODEL), lambda i: (0, 0)),
            pl.BlockSpec((D_MODEL, Z_COLS), lambda i: (0, 0)),
        ],
        out_specs=[
            pl.BlockSpec((TM, Z_COLS), lambda i: (i, 0)),
            pl.BlockSpec((TM, 128), lambda i: (i, 0)),
        ],
        out_shape=[
            jax.ShapeDtypeStruct((n, Z_COLS), F32),
            jax.ShapeDtypeStruct((n, 128), F32),
        ],
        compiler_params=_cparams(("parallel",)),
        name="inproj",
    )(x2, g, w)


def _sgu_kernel(z_ref, lng_ref, w_ref, bias_ref, o_ref):
    row = lax.broadcasted_iota(jnp.int32, (SG_CHUNK, 4 * SG_CHUNK), 0)
    col = lax.broadcasted_iota(jnp.int32, (SG_CHUNK, 4 * SG_CHUNK), 1) % SG_CHUNK
    w = jnp.where(row >= col, w_ref[...], 0.0).astype(BF16)
    lane_head = lax.broadcasted_iota(jnp.int32, (SG_CHUNK, GW), 1) // HEAD_DIM
    lng = lng_ref[...]
    bias = bias_ref[...]
    for c in range(TS // SG_CHUNK):
        blk = z_ref[0, c * SG_CHUNK:(c + 1) * SG_CHUNK, :]
        u = blk[:, :GW]
        v = blk[:, GW:]
        mu = jnp.mean(v, axis=-1, keepdims=True)
        d = v - mu
        var = jnp.mean(d * d, axis=-1, keepdims=True)
        vn = ((d * lax.rsqrt(var + LN_EPS)) * lng).astype(BF16)
        zero = jnp.zeros_like(vn)
        v4 = jnp.concatenate([jnp.where(lane_head == h, vn, zero) for h in range(N_HEADS)], axis=0)
        sv = _dot(w, v4) + bias
        o_ref[0, c * SG_CHUNK:(c + 1) * SG_CHUNK, :] = (u * sv).astype(BF16)


def _sgu(z3, lng, wcat, bias):
    b, s, _ = z3.shape
    return pl.pallas_call(
        _sgu_kernel,
        grid=(b, s // TS),
        in_specs=[
            pl.BlockSpec((1, TS, 2 * GW), lambda bi, si: (bi, si, 0)),
            pl.BlockSpec((1, GW), lambda bi, si: (0, 0)),
            pl.BlockSpec((SG_CHUNK, 4 * SG_CHUNK), lambda bi, si: (0, 0)),
            pl.BlockSpec((SG_CHUNK, GW), lambda bi, si: (0, 0)),
        ],
        out_specs=pl.BlockSpec((1, TS, GW), lambda bi, si: (bi, si, 0)),
        out_shape=jax.ShapeDtypeStruct((b, s, GW), BF16),
        compiler_params=_cparams(("parallel", "parallel")),
        name="mix_sgu",
    )(z3, lng, wcat, bias)


def _conv_kernel(z_ref, halo_ref, cw_ref, cb_ref, lng_ref, lnb_ref, pw_ref, pwb_ref, o_ref, hbuf):
    si = pl.program_id(1)
    hal = halo_ref[0]
    hh = hal[:, :GW] * jax.nn.sigmoid(hal[:, GW:])
    hbuf[0:CONV_HALO, :] = jnp.where(si > 0, hh, 0.0)
    blk = z_ref[0]
    hbuf[CONV_HALO:CONV_HALO + TS, :] = blk[:, :GW] * jax.nn.sigmoid(blk[:, GW:])
    cw = cw_ref[...]
    cb = cb_ref[...]
    lng = lng_ref[...]
    lnb = lnb_ref[...]
    pw = pw_ref[...]
    pwb = pwb_ref[...]
    lead = CONV_HALO - (CONV_WIDTH - 1)
    for r0 in range(0, TS, SUB):
        acc = jnp.zeros((SUB, GW), F32)
        for k in range(CONV_WIDTH):
            acc = acc + cw[k:k + 1, :] * hbuf[r0 + lead + k:r0 + lead + k + SUB, :]
        acc = acc + cb
        mu = jnp.mean(acc, axis=-1, keepdims=True)
        d = acc - mu
        var = jnp.mean(d * d, axis=-1, keepdims=True)
        y = (d * lax.rsqrt(var + LN_EPS)) * lng + lnb
        y = y * jax.nn.sigmoid(y)
        out = _dot(y.astype(BF16), pw) + pwb
        o_ref[0, r0:r0 + SUB, :] = out.astype(BF16)


def _conv(z3, cw, cb, lng, lnb, pw, pwb):
    b, s, _ = z3.shape
    ratio = TS // CONV_HALO
    vec = lambda n: pl.BlockSpec((1, n), lambda bi, si: (0, 0))
    return pl.pallas_call(
        _conv_kernel,
        grid=(b, s // TS),
        in_specs=[
            pl.BlockSpec((1, TS, 2 * GW), lambda bi, si: (bi, si, 1)),
            pl.BlockSpec((1, CONV_HALO, 2 * GW), lambda bi, si: (bi, jnp.maximum(si * ratio - 1, 0), 1)),
            pl.BlockSpec((CONV_WIDTH, GW), lambda bi, si: (0, 0)),
            vec(GW), vec(GW), vec(GW),
            pl.BlockSpec((GW, GW), lambda bi, si: (0, 0)),
            vec(GW),
        ],
        out_specs=pl.BlockSpec((1, TS, GW), lambda bi, si: (bi, si, 0)),
        out_shape=jax.ShapeDtypeStruct((b, s, GW), BF16),
        scratch_shapes=[pltpu.VMEM((CONV_HALO + TS, GW), F32)],
        compiler_params=_cparams(("parallel", "parallel")),
        name="mix_conv",
    )(z3, z3, cw, cb, lng, lnb, pw, pwb)


def _pool_kernel(z_ref, halo_ref, w_ref, scale_ref, o_ref, xbuf):
    si = pl.program_id(1)
    xbuf[0:POOL_HALO, :] = jnp.where(si > 0, halo_ref[0], 0.0)
    xbuf[POOL_HALO:POOL_HALO + TS, :] = z_ref[0]
    lane_grp = lax.broadcasted_iota(jnp.int32, (1, GW), 1) // (GW // len(POOL_WINDOWS))
    win = jnp.zeros((1, GW), jnp.int32)
    for gi, wlen in enumerate(POOL_WINDOWS):
        win = jnp.where(lane_grp == gi, wlen, win)
    w = w_ref[...]
    scale = scale_ref[...]
    for r0 in range(0, TS, SUB):
        x = xbuf[POOL_HALO + r0:POOL_HALO + r0 + SUB, :]
        acc = x
        for k in range(1, max(POOL_WINDOWS)):
            tap = xbuf[POOL_HALO + r0 - k:POOL_HALO + r0 - k + SUB, :]
            acc = acc + jnp.where(win > k, tap, 0.0)
        t1 = si * TS + r0 + 1 + lax.broadcasted_iota(jnp.int32, (SUB, GW), 0)
        cnt = jnp.minimum(t1, win).astype(F32)
        diff = acc / cnt - x
        out = _dot(diff.astype(BF16), w) * scale
        o_ref[0, r0:r0 + SUB, :] = out.astype(BF16)


def _pool(z3, wbd, scale):
    b, s, _ = z3.shape
    ratio = TS // POOL_HALO
    col = (KV_OFF - GW) // GW
    return pl.pallas_call(
        _pool_kernel,
        grid=(b, s // TS),
        in_specs=[
            pl.BlockSpec((1, TS, GW), lambda bi, si: (bi, si, col)),
            pl.BlockSpec((1, POOL_HALO, GW), lambda bi, si: (bi, jnp.maximum(si * ratio - 1, 0), col)),
            pl.BlockSpec((GW, GW), lambda bi, si: (0, 0)),
            pl.BlockSpec((1, GW), lambda bi, si: (0, 0)),
        ],
        out_specs=pl.BlockSpec((1, TS, GW), lambda bi, si: (bi, si, 0)),
        out_shape=jax.ShapeDtypeStruct((b, s, GW), BF16),
        scratch_shapes=[pltpu.VMEM((POOL_HALO + TS, GW), F32)],
        compiler_params=_cparams(("parallel", "parallel")),
        name="mix_pool",
    )(z3, z3, wbd, scale)


def _compress_kernel(x_ref, pe0_ref, pe1_ref, w0_ref, w1_ref, w2_ref, o_ref):
    m = CMP_TB * N_CMP_PAD
    x = x_ref[...].reshape(m, CMP_STRIDE * 128)
    p0 = _dot((x + pe0_ref[...]).astype(BF16), w0_ref[...])
    p1 = _dot((x + pe1_ref[...]).astype(BF16), w1_ref[...])
    pre = p0 + pltpu.roll(p1, m - 1, axis=0)
    hid = pre * jax.nn.sigmoid(pre)
    out = _dot(hid.astype(BF16), w2_ref[...])
    rowid = lax.broadcasted_iota(jnp.int32, (m, 128), 0) % N_CMP_PAD
    out = jnp.where(rowid < N_CMP_PAD - 1, out, 0.0)
    o_ref[...] = out.reshape(CMP_TB, N_CMP_PAD, 128)


def _compress(kvc3, pe0, pe1, w0, w1, w2):
    b = kvc3.shape[0]
    width = CMP_STRIDE * 128
    return pl.pallas_call(
        _compress_kernel,
        grid=(b // CMP_TB,),
        in_specs=[
            pl.BlockSpec((CMP_TB, N_CMP_PAD, width), lambda i: (i, 0, 0)),
            pl.BlockSpec((1, width), lambda i: (0, 0)),
            pl.BlockSpec((1, width), lambda i: (0, 0)),
            pl.BlockSpec((width, 128), lambda i: (0, 0)),
            pl.BlockSpec((width, 128), lambda i: (0, 0)),
            pl.BlockSpec((128, 128), lambda i: (0, 0)),
        ],
        out_specs=pl.BlockSpec((CMP_TB, N_CMP_PAD, 128), lambda i: (i, 0, 0)),
        out_shape=jax.ShapeDtypeStruct((b, N_CMP_PAD, 128), F32),
        compiler_params=_cparams(("parallel",)),
        name="nsa_compress",
    )(kvc3, pe0, pe1, w0, w1, w2)


def _nsa_kernel(q_ref, kv_ref, kcv_ref, o_ref, ks, vs, kwp, vwp, m_sc, l_sc, acc_sc):
    i = pl.program_id(1)
    seq = kv_ref.shape[1]
    nq = Q_BLOCK
    hq = N_HEADS * nq

    @pl.when(i == 0)
    def _():
        kwp[0:WIN, :] = jnp.zeros((WIN, HEAD_DIM), BF16)
        vwp[0:WIN, :] = jnp.zeros((WIN, HEAD_DIM), BF16)
        for r in range(0, seq, 256):
            blk = kv_ref[0, r:r + 256, :]
            ks[r:r + 256, :] = blk[:, 128:192].astype(BF16)
            vs[r:r + 256, :] = blk[:, 192:256].astype(BF16)
            kwp[WIN + r:WIN + r + 256, :] = blk[:, 256:320].astype(BF16)
            vwp[WIN + r:WIN + r + 256, :] = blk[:, 320:384].astype(BF16)

    q0 = pl.multiple_of(i * nq, nq)
    q = q_ref[0] * (HEAD_DIM ** -0.5)
    q4 = jnp.concatenate([q[:, h * HEAD_DIM:(h + 1) * HEAD_DIM] for h in range(N_HEADS)],
                         axis=0).astype(BF16)
    trow = q0 + lax.broadcasted_iota(jnp.int32, (nq, 1), 0)

    kcv = kcv_ref[0]
    kc = kcv[:, :HEAD_DIM].astype(BF16)
    vc = kcv[:, HEAD_DIM:].astype(BF16)
    s = _nt(q4, kc).reshape(N_HEADS, nq, N_CMP_PAD)
    cidx = lax.broadcasted_iota(jnp.int32, (nq, N_CMP_PAD), 1)
    cmask = (cidx * CMP_STRIDE + (CMP_BLOCK - 1) <= trow) & (cidx < N_CMP_PAD - 1)
    sm = jnp.where(cmask[None], s, NEG)
    e = jnp.exp(sm - jnp.max(sm, axis=-1, keepdims=True))
    p = e / jnp.sum(e, axis=-1, keepdims=True)
    p = jnp.where((trow >= CMP_BLOCK - 1)[None], p, 0.0)
    o_cmp = _dot(p.reshape(hq, N_CMP_PAD).astype(BF16), vc)

    psum = jnp.sum(p, axis=0)
    p_hi = psum.astype(BF16)
    p_lo = (psum - p_hi.astype(F32)).astype(BF16)
    jrow = lax.broadcasted_iota(jnp.int32, (N_SLC, N_CMP_PAD), 0)
    ccol = lax.broadcasted_iota(jnp.int32, (N_SLC, N_CMP_PAD), 1)
    ov = ((ccol * CMP_STRIDE <= jrow * SLC_BLOCK + (SLC_BLOCK - 1))
          & (ccol * CMP_STRIDE + (CMP_BLOCK - 1) >= jrow * SLC_BLOCK)
          & (ccol < N_CMP_PAD - 1))
    ov = jnp.where(ov, 1.0, 0.0).astype(BF16)
    imp = _nt(ov, p_hi) + _nt(ov, p_lo)
    jj = lax.broadcasted_iota(jnp.int32, (N_SLC, nq), 0)
    tt = q0 + lax.broadcasted_iota(jnp.int32, (N_SLC, nq), 1)
    cur = tt // SLC_BLOCK
    valid = jj * SLC_BLOCK <= tt
    forced = (jj == 0) | (jj == cur) | (jj == cur - 1)
    score = jnp.where(valid, imp + jnp.where(forced, FORCE_BONUS, 0.0), NEG)
    rank = jnp.zeros((N_SLC, nq), F32)
    for j2 in range(N_SLC):
        other = score[j2:j2 + 1, :]
        beats = (other > score) | ((other == score) & (jj > j2))
        rank = rank + jnp.where(beats, 1.0, 0.0)
    sel_t = jnp.where((rank < SLC_TOPK) & (score > NEG / 2), 1.0, 0.0)
    sel_t = jnp.concatenate([sel_t, jnp.zeros((128 - N_SLC, nq), F32)], axis=0).astype(BF16)
    eye = jnp.where(lax.broadcasted_iota(jnp.int32, (nq, nq), 0)
                    == lax.broadcasted_iota(jnp.int32, (nq, nq), 1), 1.0, 0.0).astype(BF16)
    sel = _nt(eye, sel_t).astype(BF16)

    m_sc[...] = jnp.full((hq, 1), NEG, F32)
    l_sc[...] = jnp.zeros((hq, 1), F32)
    acc_sc[...] = jnp.zeros((hq, HEAD_DIM), F32)

    def slc_step(kt, carry):
        k0 = pl.multiple_of(kt * SLC_TK, SLC_TK)
        kt_ = ks[pl.ds(k0, SLC_TK), :]
        vt_ = vs[pl.ds(k0, SLC_TK), :]
        sc = _nt(q4, kt_).reshape(N_HEADS, nq, SLC_TK)
        key = k0 + lax.broadcasted_iota(jnp.int32, (128, SLC_TK), 1)
        blk_of_key = jnp.where(key // SLC_BLOCK == lax.broadcasted_iota(jnp.int32, (128, SLC_TK), 0),
                               1.0, 0.0).astype(BF16)
        chosen = _dot(sel, blk_of_key)
        ok = ((chosen > 0.5) & (key[:nq] <= trow))[None]
        sc = jnp.where(ok, sc, NEG)
        m_old = m_sc[...].reshape(N_HEADS, nq, 1)
        m_new = jnp.maximum(m_old, jnp.max(sc, axis=-1, keepdims=True))
        pr = jnp.where(ok, jnp.exp(sc - m_new), 0.0)
        alpha = jnp.exp(m_old - m_new).reshape(hq, 1)
        l_sc[...] = alpha * l_sc[...] + jnp.sum(pr, axis=-1, keepdims=True).reshape(hq, 1)
        acc_sc[...] = alpha * acc_sc[...] + _dot(pr.reshape(hq, SLC_TK).astype(BF16), vt_)
        m_sc[...] = m_new.reshape(hq, 1)
        return carry

    n_kt = (q0 + nq + SLC_TK - 1) // SLC_TK
    lax.fori_loop(0, n_kt, slc_step, 0)
    o_slc = acc_sc[...] / l_sc[...]

    nk = WIN + nq
    kb = kwp[pl.ds(q0, nk), :]
    vb = vwp[pl.ds(q0, nk), :]
    sw = _nt(q4, kb).reshape(N_HEADS, nq, nk)
    colw = lax.broadcasted_iota(jnp.int32, (nq, nk), 1)
    rel = colw - WIN - lax.broadcasted_iota(jnp.int32, (nq, nk), 0)
    okw = ((rel <= 0) & (rel > -WIN) & (colw + q0 >= WIN))[None]
    sw = jnp.where(okw, sw, NEG)
    ew = jnp.exp(sw - jnp.max(sw, axis=-1, keepdims=True))
    pw = ew / jnp.sum(ew, axis=-1, keepdims=True)
    o_win = _dot(pw.reshape(hq, nk).astype(BF16), vb)

    gl = kv_ref[0, pl.ds(q0, nq), 384:384 + 3 * N_HEADS]
    g = jax.nn.sigmoid(gl)
    gcol = lambda br: jnp.concatenate([g[:, 3 * h + br:3 * h + br + 1] for h in range(N_HEADS)], axis=0)
    o = gcol(0) * o_cmp + gcol(1) * o_slc + gcol(2) * o_win
    o_ref[0] = jnp.concatenate([o[h * nq:(h + 1) * nq, :] for h in range(N_HEADS)], axis=1).astype(BF16)


def _nsa(z3, kcv):
    b, s, _ = z3.shape
    hq = N_HEADS * Q_BLOCK
    return pl.pallas_call(
        _nsa_kernel,
        grid=(b, s // Q_BLOCK),
        in_specs=[
            pl.BlockSpec((1, Q_BLOCK, GW), lambda bi, qi: (bi, qi, 1024 // GW)),
            pl.BlockSpec((1, s, 512), lambda bi, qi: (bi, 0, KV_OFF // 512)),
            pl.BlockSpec((1, N_CMP_PAD, 128), lambda bi, qi: (bi, 0, 0)),
        ],
        out_specs=pl.BlockSpec((1, Q_BLOCK, GW), lambda bi, qi: (bi, qi, 0)),
        out_shape=jax.ShapeDtypeStruct((b, s, GW), BF16),
        scratch_shapes=[
            pltpu.VMEM((s, HEAD_DIM), BF16),
            pltpu.VMEM((s, HEAD_DIM), BF16),
            pltpu.VMEM((WIN + s, HEAD_DIM), BF16),
            pltpu.VMEM((WIN + s, HEAD_DIM), BF16),
            pltpu.VMEM((hq, 1), F32),
            pltpu.VMEM((hq, 1), F32),
            pltpu.VMEM((hq, HEAD_DIM), F32),
        ],
        compiler_params=_cparams(("parallel", "arbitrary")),
        name="nsa_attn",
    )(z3, z3, kcv)


def _outproj_kernel(ya_ref, yb_ref, yc_ref, yd_ref, x_ref, w_ref, g_ref, o_ref):
    y = jnp.concatenate([ya_ref[...], yb_ref[...], yc_ref[...], yd_ref[...]], axis=1)
    mix = _dot(y, w_ref[...])
    r = lax.rsqrt(jnp.mean(mix * mix, axis=-1, keepdims=True) + RMS_EPS)
    o_ref[...] = x_ref[...] + (mix * r) * g_ref[...]


def _outproj(ya, yb, yc, yd, x2, w, g):
    n = x2.shape[0]
    yspec = pl.BlockSpec((TM, GW), lambda i: (i, 0))
    return pl.pallas_call(
        _outproj_kernel,
        grid=(n // TM,),
        in_specs=[
            yspec, yspec, yspec, yspec,
            pl.BlockSpec((TM, D_MODEL), lambda i: (i, 0)),
            pl.BlockSpec((D_MODEL, D_MODEL), lambda i: (0, 0)),
            pl.BlockSpec((1, D_MODEL), lambda i: (0, 0)),
        ],
        out_specs=pl.BlockSpec((TM, D_MODEL), lambda i: (i, 0)),
        out_shape=jax.ShapeDtypeStruct((n, D_MODEL), F32),
        compiler_params=_cparams(("parallel",)),
        name="outproj",
    )(ya, yb, yc, yd, x2, w, g)


def _ffn_kernel(x_ref, gpre_ref, wg_ref, wu_ref, wd_ref, gpost_ref, o_ref):
    x = x_ref[...]
    r = lax.rsqrt(jnp.mean(x * x, axis=-1, keepdims=True) + RMS_EPS)
    h = ((x * r) * gpre_ref[...]).astype(BF16)
    f = jnp.zeros((TM, D_MODEL), F32)
    for c0 in range(0, FFN_HIDDEN, FFN_CHUNK):
        gate = _dot(h, wg_ref[:, c0:c0 + FFN_CHUNK])
        up = _dot(h, wu_ref[:, c0:c0 + FFN_CHUNK])
        act = ((gate * jax.nn.sigmoid(gate)) * up).astype(BF16)
        f = f + _dot(act, wd_ref[c0:c0 + FFN_CHUNK, :])
    r2 = lax.rsqrt(jnp.mean(f * f, axis=-1, keepdims=True) + RMS_EPS)
    o_ref[...] = x + (f * r2) * gpost_ref[...]


def _ffn(x2, gpre, wg, wu, wd, gpost):
    n = x2.shape[0]
    const = lambda shape: pl.BlockSpec(shape, lambda i: (0, 0), pipeline_mode=pl.Buffered(1))
    return pl.pallas_call(
        _ffn_kernel,
        grid=(n // TM,),
        in_specs=[
            pl.BlockSpec((TM, D_MODEL), lambda i: (i, 0)),
            const((1, D_MODEL)),
            const((D_MODEL, FFN_HIDDEN)),
            const((D_MODEL, FFN_HIDDEN)),
            const((FFN_HIDDEN, D_MODEL)),
            const((1, D_MODEL)),
        ],
        out_specs=pl.BlockSpec((TM, D_MODEL), lambda i: (i, 0)),
        out_shape=jax.ShapeDtypeStruct((n, D_MODEL), F32),
        compiler_params=_cparams(("parallel",)),
        name="ffn",
    )(x2, gpre, wg, wu, wd, gpost)


def _interleave_rows(wk, wv):
    z = jnp.zeros((CMP_STRIDE, HEAD_DIM, HEAD_DIM), wk.dtype)
    wk3 = wk.reshape(CMP_STRIDE, HEAD_DIM, HEAD_DIM)
    wv3 = wv.reshape(CMP_STRIDE, HEAD_DIM, HEAD_DIM)
    top = jnp.concatenate([wk3, z], axis=2)
    bot = jnp.concatenate([z, wv3], axis=2)
    return jnp.concatenate([top, bot], axis=1).reshape(CMP_STRIDE * 128, 128)


def _block_diag(mats):
    n = len(mats)
    rows = []
    for a, m in enumerate(mats):
        rows.append(jnp.concatenate([m if a == c else jnp.zeros_like(m) for c in range(n)], axis=1))
    return jnp.concatenate(rows, axis=0)


def _layer(x2, batch, seq, p):
    n = x2.shape[0]
    row = lambda v: v.reshape(1, -1)
    w_in = p["w_in"]
    w_in = jnp.concatenate(
        [w_in[:, :1280], w_in[:, 1676:1932], w_in[:, 1280:1676],
         jnp.zeros((D_MODEL, Z_COLS - 1932), w_in.dtype)], axis=1).astype(BF16)
    z, kvc = _inproj(x2, row(p["g_pre_mix"]), w_in)
    z3 = z.reshape(batch, seq, Z_COLS)

    wcat = jnp.transpose(p["sg_w"], (1, 0, 2)).reshape(SG_CHUNK, N_HEADS * SG_CHUNK)
    bias = jnp.repeat(p["sg_b"].T, HEAD_DIM, axis=1)
    y_a = _sgu(z3, row(p["sg_ln_g"]), wcat, bias)

    y_b = _conv(z3, p["cv_w"], row(p["cv_b"]), row(p["cv_ln_g"]), row(p["cv_ln_b"]),
                p["cv_pw"].astype(BF16), row(p["cv_pw_b"]))

    half = CMP_STRIDE
    pe = lambda a: jnp.concatenate([p["cmp_pos_k"][a * half:(a + 1) * half],
                                    p["cmp_pos_v"][a * half:(a + 1) * half]], axis=1).reshape(1, -1)
    wpart = lambda a: _interleave_rows(p["cmp_w1_k"][a * 1024:(a + 1) * 1024],
                                       p["cmp_w1_v"][a * 1024:(a + 1) * 1024]).astype(BF16)
    w2 = _block_diag([p["cmp_w2_k"], p["cmp_w2_v"]]).astype(BF16)
    kvc3 = kvc.reshape(batch, seq // CMP_STRIDE, CMP_STRIDE * 128)
    kcv = _compress(kvc3, pe(0), pe(1), wpart(0), wpart(1), w2)
    y_c = _nsa(z3, kcv)

    wpool = _block_diag([p["pool_w"][gi] for gi in range(len(POOL_WINDOWS))]).astype(BF16)
    y_d = _pool(z3, wpool, row(p["pool_scale"]))

    flat = lambda y: y.reshape(n, GW)
    x2 = _outproj(flat(y_a), flat(y_b), flat(y_c), flat(y_d), x2,
                  p["w_out"].astype(BF16), row(p["g_post_mix"]))
    wgu = p["ffn_w_gu"].astype(BF16)
    return _ffn(x2, row(p["g_pre_ffn"]), wgu[:, :FFN_HIDDEN], wgu[:, FFN_HIDDEN:],
                p["ffn_w_down"].astype(BF16), row(p["g_post_ffn"]))


_PARAM_NAMES = ("g_pre_mix", "g_post_mix", "g_pre_ffn", "g_post_ffn", "w_in", "sg_ln_g", "sg_w", "sg_b",
                "cv_w", "cv_b", "cv_ln_g", "cv_ln_b", "cv_pw", "cv_pw_b", "cmp_pos_k", "cmp_pos_v",
                "cmp_w1_k", "cmp_w2_k", "cmp_w1_v", "cmp_w2_v", "pool_w", "pool_scale", "w_out",
                "ffn_w_gu", "ffn_w_down")


def kernel(x, g_pre_mix, g_post_mix, g_pre_ffn, g_post_ffn, w_in, sg_ln_g, sg_w, sg_b, cv_w, cv_b, cv_ln_g, cv_ln_b, cv_pw, cv_pw_b, cmp_pos_k, cmp_pos_v, cmp_w1_k, cmp_w2_k, cmp_w1_v, cmp_w2_v, pool_w, pool_scale, w_out, ffn_w_gu, ffn_w_down):
    params = dict(zip(_PARAM_NAMES, (g_pre_mix, g_post_mix, g_pre_ffn, g_post_ffn, w_in, sg_ln_g, sg_w,
                                     sg_b, cv_w, cv_b, cv_ln_g, cv_ln_b, cv_pw, cv_pw_b, cmp_pos_k,
                                     cmp_pos_v, cmp_w1_k, cmp_w2_k, cmp_w1_v, cmp_w2_v, pool_w,
                                     pool_scale, w_out, ffn_w_gu, ffn_w_down)))
    batch, seq, _ = x.shape
    x2 = x.reshape(batch * seq, D_MODEL)
    for layer in range(g_pre_mix.shape[0]):
        x2 = _layer(x2, batch, seq, {k: v[layer] for k, v in params.items()})
    return x2.reshape(batch, seq, D_MODEL)
```

```python
import jax
import jax.numpy as jnp
from jax import lax
from jax.experimental import pallas as pl
from jax.experimental.pallas import tpu as pltpu

F32 = jnp.float32
BF16 = jnp.bfloat16

D_MODEL = 1024
GW = 256
HEAD_DIM = 64
N_HEADS = 4
SG_CHUNK = 128
CONV_WIDTH = 31
CMP_BLOCK = 32
CMP_STRIDE = 16
SLC_BLOCK = 64
SLC_TOPK = 8
WIN = 512
Q_BLOCK = 128
FORCE_BONUS = 1e4
NEG = -1e30
POOL_WINDOWS = (2, 4, 8, 16)
FFN_HIDDEN = 2816
RMS_EPS = 1e-6
LN_EPS = 1e-5
Z_COLS = 2048
KV_OFF = 1536
N_CMP_PAD = 128
N_SLC = 32

TM = 512
TS = 512
CONV_HALO = 32
POOL_HALO = 16
SUB = 64
SUBLANES = 8
POOL_PAD = SUBLANES
SHIFT_CHUNK = 128
LOG2E = 1.4426950408889634
SLC_TK = 512
AUG = 128
PAD_LANE = HEAD_DIM + N_SLC
FFN_CHUNK = 1408
CMP_TB = 4
VMEM_LIMIT = 56 * 1024 * 1024


def _cparams(sem):
    return pltpu.CompilerParams(dimension_semantics=sem, vmem_limit_bytes=VMEM_LIMIT)


def _nt(a, b):
    return lax.dot_general(a, b, (((1,), (1,)), ((), ())), preferred_element_type=F32)


def _dot(a, b):
    return jnp.dot(a, b, preferred_element_type=F32)


def _inproj_kernel(x_ref, g_ref, w_ref, z_ref, kvc_ref):
    x = x_ref[...]
    r = lax.rsqrt(jnp.mean(x * x, axis=-1, keepdims=True) + RMS_EPS)
    h = ((x * r) * g_ref[...]).astype(BF16)
    z = _dot(h, w_ref[...])
    z_ref[...] = z
    kvc_ref[...] = z[:, KV_OFF:KV_OFF + 128]


def _inproj(x2, g, w):
    n = x2.shape[0]
    return pl.pallas_call(
        _inproj_kernel,
        grid=(n // TM,),
        in_specs=[
            pl.BlockSpec((TM, D_MODEL), lambda i: (i, 0)),
            pl.BlockSpec((1, D_MODEL), lambda i: (0, 0)),
            pl.BlockSpec((D_MODEL, Z_COLS), lambda i: (0, 0)),
        ],
        out_specs=[
            pl.BlockSpec((TM, Z_COLS), lambda i: (i, 0)),
            pl.BlockSpec((TM, 128), lambda i: (i, 0)),
        ],
        out_shape=[
            jax.ShapeDtypeStruct((n, Z_COLS), F32),
            jax.ShapeDtypeStruct((n, 128), F32),
        ],
        compiler_params=_cparams(("parallel",)),
        name="inproj",
    )(x2, g, w)


def _sgu_kernel(z_ref, lng_ref, w_ref, bias_ref, o_ref):
    row = lax.broadcasted_iota(jnp.int32, (SG_CHUNK, 4 * SG_CHUNK), 0)
    col = lax.broadcasted_iota(jnp.int32, (SG_CHUNK, 4 * SG_CHUNK), 1) % SG_CHUNK
    w = jnp.where(row >= col, w_ref[...], 0.0).astype(BF16)
    lane_head = lax.broadcasted_iota(jnp.int32, (SG_CHUNK, GW), 1) // HEAD_DIM
    lng = lng_ref[...]
    bias = bias_ref[...]
    for c in range(TS // SG_CHUNK):
        blk = z_ref[0, c * SG_CHUNK:(c + 1) * SG_CHUNK, :]
        u = blk[:, :GW]
        v = blk[:, GW:]
        mu = jnp.mean(v, axis=-1, keepdims=True)
        d = v - mu
        var = jnp.mean(d * d, axis=-1, keepdims=True)
        vn = ((d * lax.rsqrt(var + LN_EPS)) * lng).astype(BF16)
        zero = jnp.zeros_like(vn)
        v4 = jnp.concatenate([jnp.where(lane_head == h, vn, zero) for h in range(N_HEADS)], axis=0)
        sv = _dot(w, v4) + bias
        o_ref[0, c * SG_CHUNK:(c + 1) * SG_CHUNK, :] = (u * sv).astype(BF16)


def _sgu(z3, lng, wcat, bias):
    b, s, _ = z3.shape
    return pl.pallas_call(
        _sgu_kernel,
        grid=(b, s // TS),
        in_specs=[
            pl.BlockSpec((1, TS, 2 * GW), lambda bi, si: (bi, si, 0)),
            pl.BlockSpec((1, GW), lambda bi, si: (0, 0)),
            pl.BlockSpec((SG_CHUNK, 4 * SG_CHUNK), lambda bi, si: (0, 0)),
            pl.BlockSpec((SG_CHUNK, GW), lambda bi, si: (0, 0)),
        ],
        out_specs=pl.BlockSpec((1, TS, GW), lambda bi, si: (bi, si, 0)),
        out_shape=jax.ShapeDtypeStruct((b, s, GW), BF16),
        compiler_params=_cparams(("parallel", "parallel")),
        name="mix_sgu",
    )(z3, lng, wcat, bias)


def _conv_kernel(z_ref, halo_ref, cw_ref, cb_ref, lng_ref, lnb_ref, pw_ref, pwb_ref, o_ref, hbuf):
    si = pl.program_id(1)
    hal = halo_ref[0]
    hh = hal[:, :GW] * jax.nn.sigmoid(hal[:, GW:])
    hbuf[0, 0:CONV_HALO, :] = jnp.where(si > 0, hh, 0.0)
    for r0 in range(0, TS, SUB):
        blk = z_ref[0, r0:r0 + SUB, :]
        hbuf[0, CONV_HALO + r0:CONV_HALO + r0 + SUB, :] = blk[:, :GW] * jax.nn.sigmoid(blk[:, GW:])
    rows = CONV_HALO + TS - SUBLANES
    for r in range(1, SUBLANES):
        for j0 in range(0, rows, SHIFT_CHUNK):
            n = min(SHIFT_CHUNK, rows - j0)
            hbuf[r, j0:j0 + n, :] = hbuf[0, j0 + r:j0 + r + n, :]
    cw = cw_ref[...]
    cb = cb_ref[...]
    lng = lng_ref[...]
    lnb = lnb_ref[...]
    pw = pw_ref[...]
    pwb = pwb_ref[...]
    lead = CONV_HALO - (CONV_WIDTH - 1)
    for r0 in range(0, TS, SUB):
        acc = jnp.zeros((SUB, GW), F32)
        for k in range(CONV_WIDTH):
            off = lead + k
            base = r0 + off - off % SUBLANES
            acc = acc + cw[k:k + 1, :] * hbuf[off % SUBLANES, base:base + SUB, :]
        acc = acc + cb
        mu = jnp.mean(acc, axis=-1, keepdims=True)
        d = acc - mu
        var = jnp.mean(d * d, axis=-1, keepdims=True)
        y = (d * lax.rsqrt(var + LN_EPS)) * lng + lnb
        y = y * jax.nn.sigmoid(y)
        out = _dot(y.astype(BF16), pw) + pwb
        o_ref[0, r0:r0 + SUB, :] = out.astype(BF16)


def _conv(z3, cw, cb, lng, lnb, pw, pwb):
    b, s, _ = z3.shape
    ratio = TS // CONV_HALO
    vec = lambda n: pl.BlockSpec((1, n), lambda bi, si: (0, 0))
    return pl.pallas_call(
        _conv_kernel,
        grid=(b, s // TS),
        in_specs=[
            pl.BlockSpec((1, TS, 2 * GW), lambda bi, si: (bi, si, 1)),
            pl.BlockSpec((1, CONV_HALO, 2 * GW), lambda bi, si: (bi, jnp.maximum(si * ratio - 1, 0), 1)),
            pl.BlockSpec((CONV_WIDTH, GW), lambda bi, si: (0, 0)),
            vec(GW), vec(GW), vec(GW),
            pl.BlockSpec((GW, GW), lambda bi, si: (0, 0)),
            vec(GW),
        ],
        out_specs=pl.BlockSpec((1, TS, GW), lambda bi, si: (bi, si, 0)),
        out_shape=jax.ShapeDtypeStruct((b, s, GW), BF16),
        scratch_shapes=[pltpu.VMEM((SUBLANES, CONV_HALO + TS, GW), F32)],
        compiler_params=_cparams(("parallel", "parallel")),
        name="mix_conv",
    )(z3, z3, cw, cb, lng, lnb, pw, pwb)


def _pool_kernel(z_ref, halo_ref, w_ref, scale_ref, o_ref, xbuf, s2buf, s4buf, s8buf):
    si = pl.program_id(1)
    data0 = POOL_PAD + POOL_HALO
    total = data0 + TS
    zeros = jnp.zeros((POOL_PAD, GW), F32)
    xbuf[0:POOL_PAD, :] = zeros
    s2buf[0:POOL_PAD, :] = zeros
    s4buf[0:POOL_PAD, :] = zeros
    xbuf[POOL_PAD:data0, :] = jnp.where(si > 0, halo_ref[0], 0.0)
    xbuf[data0:total, :] = z_ref[0]
    for src, dst, shift in ((xbuf, s2buf, 1), (s2buf, s4buf, 2), (s4buf, s8buf, 4)):
        for j0 in range(POOL_PAD, total, SHIFT_CHUNK):
            n = min(SHIFT_CHUNK, total - j0)
            dst[j0:j0 + n, :] = src[j0:j0 + n, :] + src[j0 - shift:j0 - shift + n, :]
    lane_grp = lax.broadcasted_iota(jnp.int32, (1, GW), 1) // (GW // len(POOL_WINDOWS))
    win = jnp.zeros((1, GW), jnp.int32)
    for gi, wlen in enumerate(POOL_WINDOWS):
        win = jnp.where(lane_grp == gi, wlen, win)
    w = w_ref[...]
    scale = scale_ref[...]
    for r0 in range(0, TS, SUB):
        j = data0 + r0
        x = xbuf[j:j + SUB, :]
        s8 = s8buf[j:j + SUB, :]
        s16 = s8 + s8buf[j - 8:j - 8 + SUB, :]
        acc = jnp.where(lane_grp == 0, s2buf[j:j + SUB, :],
                        jnp.where(lane_grp == 1, s4buf[j:j + SUB, :],
                                  jnp.where(lane_grp == 2, s8, s16)))
        t1 = si * TS + r0 + 1 + lax.broadcasted_iota(jnp.int32, (SUB, GW), 0)
        cnt = jnp.minimum(t1, win).astype(F32)
        diff = acc / cnt - x
        out = _dot(diff.astype(BF16), w) * scale
        o_ref[0, r0:r0 + SUB, :] = out.astype(BF16)


def _pool(z3, wbd, scale):
    b, s, _ = z3.shape
    ratio = TS // POOL_HALO
    col = (KV_OFF - GW) // GW
    return pl.pallas_call(
        _pool_kernel,
        grid=(b, s // TS),
        in_specs=[
            pl.BlockSpec((1, TS, GW), lambda bi, si: (bi, si, col)),
            pl.BlockSpec((1, POOL_HALO, GW), lambda bi, si: (bi, jnp.maximum(si * ratio - 1, 0), col)),
            pl.BlockSpec((GW, GW), lambda bi, si: (0, 0)),
            pl.BlockSpec((1, GW), lambda bi, si: (0, 0)),
        ],
        out_specs=pl.BlockSpec((1, TS, GW), lambda bi, si: (bi, si, 0)),
        out_shape=jax.ShapeDtypeStruct((b, s, GW), BF16),
        scratch_shapes=[pltpu.VMEM((POOL_PAD + POOL_HALO + TS, GW), F32)] * 4,
        compiler_params=_cparams(("parallel", "parallel")),
        name="mix_pool",
    )(z3, z3, wbd, scale)


def _compress_kernel(x_ref, pe0_ref, pe1_ref, w0_ref, w1_ref, w2_ref, o_ref):
    m = CMP_TB * N_CMP_PAD
    x = x_ref[...].reshape(m, CMP_STRIDE * 128)
    p0 = _dot((x + pe0_ref[...]).astype(BF16), w0_ref[...])
    p1 = _dot((x + pe1_ref[...]).astype(BF16), w1_ref[...])
    pre = p0 + pltpu.roll(p1, m - 1, axis=0)
    hid = pre * jax.nn.sigmoid(pre)
    out = _dot(hid.astype(BF16), w2_ref[...])
    rowid = lax.broadcasted_iota(jnp.int32, (m, 128), 0) % N_CMP_PAD
    out = jnp.where(rowid < N_CMP_PAD - 1, out, 0.0)
    o_ref[...] = out.reshape(CMP_TB, N_CMP_PAD, 128)


def _compress(kvc3, pe0, pe1, w0, w1, w2):
    b = kvc3.shape[0]
    width = CMP_STRIDE * 128
    return pl.pallas_call(
        _compress_kernel,
        grid=(b // CMP_TB,),
        in_specs=[
            pl.BlockSpec((CMP_TB, N_CMP_PAD, width), lambda i: (i, 0, 0)),
            pl.BlockSpec((1, width), lambda i: (0, 0)),
            pl.BlockSpec((1, width), lambda i: (0, 0)),
            pl.BlockSpec((width, 128), lambda i: (0, 0)),
            pl.BlockSpec((width, 128), lambda i: (0, 0)),
            pl.BlockSpec((128, 128), lambda i: (0, 0)),
        ],
        out_specs=pl.BlockSpec((CMP_TB, N_CMP_PAD, 128), lambda i: (i, 0, 0)),
        out_shape=jax.ShapeDtypeStruct((b, N_CMP_PAD, 128), F32),
        compiler_params=_cparams(("parallel",)),
        name="nsa_compress",
    )(kvc3, pe0, pe1, w0, w1, w2)


def _nsa_kernel(q_ref, kv_ref, kcv_ref, o_ref, ks_aug, kts, vs_aug, kw_aug, vw_aug, gsel):
    i = pl.program_id(1)
    seq = kv_ref.shape[1]
    nq = Q_BLOCK

    @pl.when(i == 0)
    def _():
        eye = jnp.where(lax.broadcasted_iota(jnp.int32, (AUG, AUG), 0)
                        == lax.broadcasted_iota(jnp.int32, (AUG, AUG), 1), 1.0, 0.0).astype(BF16)
        lane_p = lax.broadcasted_iota(jnp.int32, (WIN, AUG), 1)
        kw_aug[0:WIN, :] = jnp.where(lane_p == PAD_LANE, 1.0, 0.0).astype(BF16)
        vw_aug[0:WIN, :] = jnp.zeros((WIN, 2 * AUG), BF16)
        gsel[...] = jnp.where(lax.broadcasted_iota(jnp.int32, gsel.shape, 0)
                              == lax.broadcasted_iota(jnp.int32, gsel.shape, 1) // AUG, 1.0, 0.0).astype(BF16)
        lane = lax.broadcasted_iota(jnp.int32, (256, AUG), 1)
        ones = jnp.ones((256, AUG), BF16)
        for r in range(0, seq, 256):
            blk = kv_ref[0, r:r + 256, :]
            t_s = blk[:, 128:256]
            t_w = blk[:, 256:384]
            key_blk = (r + lax.broadcasted_iota(jnp.int32, (256, AUG), 0)) // SLC_BLOCK
            onehot = jnp.where(lane - HEAD_DIM == key_blk, 1.0, 0.0)
            ks_aug[r:r + 256, :] = jnp.where(lane < HEAD_DIM, t_s, onehot).astype(BF16)
            v_s = jnp.where(lane < HEAD_DIM, pltpu.roll(t_s, HEAD_DIM, axis=1), 0.0).astype(BF16)
            vs_aug[r:r + 256, :] = jnp.concatenate([v_s, ones], axis=1)
            kw_aug[WIN + r:WIN + r + 256, :] = jnp.where(lane < HEAD_DIM, t_w, 0.0).astype(BF16)
            v_w = jnp.where(lane < HEAD_DIM, pltpu.roll(t_w, HEAD_DIM, axis=1), 0.0).astype(BF16)
            vw_aug[WIN + r:WIN + r + 256, :] = jnp.concatenate([v_w, ones], axis=1)
        for c in range(0, seq, SLC_TK):
            kts[:, c:c + SLC_TK] = _nt(eye, ks_aug[c:c + SLC_TK, :]).astype(BF16)

    per = SLC_TK // nq
    n_dyn = (i + per - 1) // per
    for n_kt in range(seq // SLC_TK + 1):
        @pl.when(n_dyn == n_kt)
        def _(n_kt=n_kt):
            _nsa_tile(n_kt, i, q_ref, kv_ref, kcv_ref, o_ref, ks_aug, kts, vs_aug, kw_aug, vw_aug, gsel)


def _nsa_tile(n_kt, i, q_ref, kv_ref, kcv_ref, o_ref, ks_aug, kts, vs_aug, kw_aug, vw_aug, gsel):
    nq = Q_BLOCK
    hq = N_HEADS * nq
    eye = jnp.where(lax.broadcasted_iota(jnp.int32, (AUG, AUG), 0)
                    == lax.broadcasted_iota(jnp.int32, (AUG, AUG), 1), 1.0, 0.0).astype(BF16)
    q0 = pl.multiple_of(i * nq, nq)
    q = q_ref[0] * (HEAD_DIM ** -0.5 * LOG2E)
    lane_q = lax.broadcasted_iota(jnp.int32, (nq, AUG), 1)

    def stack_heads(extra):
        tiles = []
        for h in range(N_HEADS):
            t = q[:, AUG * (h // 2):AUG * (h // 2 + 1)]
            if h % 2:
                t = pltpu.roll(t, HEAD_DIM, axis=1)
            tiles.append(jnp.where(lane_q < HEAD_DIM, t, extra))
        return jnp.concatenate(tiles, axis=0).astype(BF16)

    pad_bias = jnp.where(lane_q == PAD_LANE, NEG, 0.0)
    qw = stack_heads(pad_bias)
    trow = q0 + lax.broadcasted_iota(jnp.int32, (nq, 1), 0)
    row_l = lax.broadcasted_iota(jnp.int32, (nq, nq), 0)
    col_l = lax.broadcasted_iota(jnp.int32, (nq, nq), 1)
    tri_le = (col_l <= row_l)[None]
    tri_gt = (col_l > row_l)[None]

    nk = WIN + nq
    sw = _nt(qw, kw_aug[pl.ds(q0, nk), :]).reshape(N_HEADS, nq, nk)
    sw = jnp.concatenate([jnp.where(tri_gt, sw[:, :, :nq], NEG), sw[:, :, nq:WIN],
                          jnp.where(tri_le, sw[:, :, WIN:], NEG)], axis=-1)
    pw = jnp.exp2(sw - jnp.max(sw, axis=-1, keepdims=True)).astype(BF16)
    o_win = _dot(pw.reshape(hq, nk), vw_aug[pl.ds(q0, nk), :])

    sd = _nt(qw, ks_aug[pl.ds(q0, nq), :]).reshape(N_HEADS, nq, nq)
    sd = jnp.where(tri_le, sd, NEG).reshape(hq, nq)
    md = jnp.max(sd, axis=-1, keepdims=True)

    kcv = kcv_ref[0]
    lane_c = lax.broadcasted_iota(jnp.int32, (N_CMP_PAD, AUG), 1)
    kc = jnp.where(lane_c < HEAD_DIM, kcv, 0.0).astype(BF16)
    vc = jnp.where(lane_c < HEAD_DIM, pltpu.roll(kcv, HEAD_DIM, axis=1), 0.0).astype(BF16)
    s = _nt(qw, kc).reshape(N_HEADS, nq, N_CMP_PAD)
    cidx = lax.broadcasted_iota(jnp.int32, (nq, N_CMP_PAD), 1)
    cmask = (cidx * CMP_STRIDE + (CMP_BLOCK - 1) <= trow) & (cidx < N_CMP_PAD - 1)
    sm = jnp.where(cmask[None], s, NEG)
    e = jnp.exp2(sm - jnp.max(sm, axis=-1, keepdims=True))
    p = e / jnp.sum(e, axis=-1, keepdims=True)
    p = jnp.where((trow >= CMP_BLOCK - 1)[None], p, 0.0)
    o_cmp = _dot(p.reshape(hq, N_CMP_PAD).astype(BF16), vc)

    psum = jnp.sum(p, axis=0)
    p_hi = psum.astype(BF16)
    p_lo = (psum - p_hi.astype(F32)).astype(BF16)
    jrow = lax.broadcasted_iota(jnp.int32, (N_SLC, N_CMP_PAD), 0)
    ccol = lax.broadcasted_iota(jnp.int32, (N_SLC, N_CMP_PAD), 1)
    ov = ((ccol * CMP_STRIDE <= jrow * SLC_BLOCK + (SLC_BLOCK - 1))
          & (ccol * CMP_STRIDE + (CMP_BLOCK - 1) >= jrow * SLC_BLOCK)
          & (ccol < N_CMP_PAD - 1))
    ov = jnp.where(ov, 1.0, 0.0).astype(BF16)
    imp = _nt(ov, p_hi) + _nt(ov, p_lo)
    jj = lax.broadcasted_iota(jnp.int32, (N_SLC, nq), 0)
    tt = q0 + lax.broadcasted_iota(jnp.int32, (N_SLC, nq), 1)
    cur = tt // SLC_BLOCK
    valid = jj * SLC_BLOCK <= tt
    forced = (jj == 0) | (jj == cur) | (jj == cur - 1)
    score = jnp.where(valid, imp + jnp.where(forced, FORCE_BONUS, 0.0), NEG)
    rank = jnp.zeros((N_SLC, nq), F32)
    for j2 in range(N_SLC):
        other = score[j2:j2 + 1, :]
        beats = (other > score) | ((other == score) & (jj > j2))
        rank = rank + jnp.where(beats, 1.0, 0.0)
    sel_t = jnp.where((rank < SLC_TOPK) & (score > NEG / 2), 1.0, 0.0)
    bias_t = jnp.where((sel_t > 0.5) & (jj < 2 * i), 0.0, NEG)
    bias_t = jnp.concatenate([jnp.zeros((HEAD_DIM, nq), F32), bias_t,
                              jnp.zeros((AUG - HEAD_DIM - N_SLC, nq), F32)], axis=0).astype(BF16)
    blk_bias = _nt(eye, bias_t)

    if n_kt:
        n_main = n_kt * SLC_TK
        qs = stack_heads(blk_bias + pad_bias)
        s_main = _dot(qs, kts[:, 0:n_main])
        m = jnp.maximum(md, jnp.max(s_main, axis=-1, keepdims=True))
        p_main = jnp.exp2(s_main - m).astype(BF16)
        p_diag = jnp.exp2(sd - m).astype(BF16)
        o_slc = _dot(p_diag, vs_aug[pl.ds(q0, nq), :]) + _dot(p_main, vs_aug[0:n_main, :])
    else:
        o_slc = _dot(jnp.exp2(sd - md).astype(BF16), vs_aug[pl.ds(q0, nq), :])

    g = jax.nn.sigmoid(kv_ref[0, pl.ds(q0, nq), 384:384 + AUG])
    g_hi = g.astype(BF16)
    g_lo = (g - g_hi.astype(F32)).astype(BF16)
    g_rep = _dot(g_hi, gsel[...]) + _dot(g_lo, gsel[...])
    gate = lambda br: jnp.concatenate(
        [g_rep[:, (3 * h + br) * AUG:(3 * h + br + 1) * AUG] for h in range(N_HEADS)], axis=0)
    o = (gate(0) * o_cmp
         + (gate(1) / o_slc[:, AUG:]) * o_slc[:, :AUG]
         + (gate(2) / o_win[:, AUG:]) * o_win[:, :AUG])
    halves = [jnp.where(lane_q < HEAD_DIM, o[2 * a * nq:(2 * a + 1) * nq, :],
                        pltpu.roll(o[(2 * a + 1) * nq:(2 * a + 2) * nq, :], HEAD_DIM, axis=1))
              for a in range(N_HEADS // 2)]
    o_ref[0] = jnp.concatenate(halves, axis=1).astype(BF16)


def _nsa(z3, kcv):
    b, s, _ = z3.shape
    return pl.pallas_call(
        _nsa_kernel,
        grid=(b, s // Q_BLOCK),
        in_specs=[
            pl.BlockSpec((1, Q_BLOCK, GW), lambda bi, qi: (bi, qi, 1024 // GW)),
            pl.BlockSpec((1, s, 512), lambda bi, qi: (bi, 0, KV_OFF // 512)),
            pl.BlockSpec((1, N_CMP_PAD, 128), lambda bi, qi: (bi, 0, 0)),
        ],
        out_specs=pl.BlockSpec((1, Q_BLOCK, GW), lambda bi, qi: (bi, qi, 0)),
        out_shape=jax.ShapeDtypeStruct((b, s, GW), BF16),
        scratch_shapes=[
            pltpu.VMEM((s, AUG), BF16),
            pltpu.VMEM((AUG, s), BF16),
            pltpu.VMEM((s, 2 * AUG), BF16),
            pltpu.VMEM((WIN + s, AUG), BF16),
            pltpu.VMEM((WIN + s, 2 * AUG), BF16),
            pltpu.VMEM((AUG, 3 * N_HEADS * AUG), BF16),
        ],
        compiler_params=_cparams(("parallel", "arbitrary")),
        name="nsa_attn",
    )(z3, z3, kcv)


def _outproj_kernel(ya_ref, yb_ref, yc_ref, yd_ref, x_ref, w_ref, g_ref, o_ref):
    y = jnp.concatenate([ya_ref[...], yb_ref[...], yc_ref[...], yd_ref[...]], axis=1)
    mix = _dot(y, w_ref[...])
    r = lax.rsqrt(jnp.mean(mix * mix, axis=-1, keepdims=True) + RMS_EPS)
    o_ref[...] = x_ref[...] + (mix * r) * g_ref[...]


def _outproj(ya, yb, yc, yd, x2, w, g):
    n = x2.shape[0]
    yspec = pl.BlockSpec((TM, GW), lambda i: (i, 0))
    return pl.pallas_call(
        _outproj_kernel,
        grid=(n // TM,),
        in_specs=[
            yspec, yspec, yspec, yspec,
            pl.BlockSpec((TM, D_MODEL), lambda i: (i, 0)),
            pl.BlockSpec((D_MODEL, D_MODEL), lambda i: (0, 0)),
            pl.BlockSpec((1, D_MODEL), lambda i: (0, 0)),
        ],
        out_specs=pl.BlockSpec((TM, D_MODEL), lambda i: (i, 0)),
        out_shape=jax.ShapeDtypeStruct((n, D_MODEL), F32),
        compiler_params=_cparams(("parallel",)),
        name="outproj",
    )(ya, yb, yc, yd, x2, w, g)


def _ffn_kernel(x_ref, gpre_ref, wg_ref, wu_ref, wd_ref, gpost_ref, o_ref):
    x = x_ref[...]
    r = lax.rsqrt(jnp.mean(x * x, axis=-1, keepdims=True) + RMS_EPS)
    h = ((x * r) * gpre_ref[...]).astype(BF16)
    f = jnp.zeros((TM, D_MODEL), F32)
    for c0 in range(0, FFN_HIDDEN, FFN_CHUNK):
        gate = _dot(h, wg_ref[:, c0:c0 + FFN_CHUNK])
        up = _dot(h, wu_ref[:, c0:c0 + FFN_CHUNK])
        act = ((gate * jax.nn.sigmoid(gate)) * up).astype(BF16)
        f = f + _dot(act, wd_ref[c0:c0 + FFN_CHUNK, :])
    r2 = lax.rsqrt(jnp.mean(f * f, axis=-1, keepdims=True) + RMS_EPS)
    o_ref[...] = x + (f * r2) * gpost_ref[...]


def _ffn(x2, gpre, wg, wu, wd, gpost):
    n = x2.shape[0]
    const = lambda shape: pl.BlockSpec(shape, lambda i: (0, 0), pipeline_mode=pl.Buffered(1))
    return pl.pallas_call(
        _ffn_kernel,
        grid=(n // TM,),
        in_specs=[
            pl.BlockSpec((TM, D_MODEL), lambda i: (i, 0)),
            const((1, D_MODEL)),
            const((D_MODEL, FFN_HIDDEN)),
            const((D_MODEL, FFN_HIDDEN)),
            const((FFN_HIDDEN, D_MODEL)),
            const((1, D_MODEL)),
        ],
        out_specs=pl.BlockSpec((TM, D_MODEL), lambda i: (i, 0)),
        out_shape=jax.ShapeDtypeStruct((n, D_MODEL), F32),
        compiler_params=_cparams(("parallel",)),
        name="ffn",
    )(x2, gpre, wg, wu, wd, gpost)


def _interleave_rows(wk, wv):
    z = jnp.zeros((CMP_STRIDE, HEAD_DIM, HEAD_DIM), wk.dtype)
    wk3 = wk.reshape(CMP_STRIDE, HEAD_DIM, HEAD_DIM)
    wv3 = wv.reshape(CMP_STRIDE, HEAD_DIM, HEAD_DIM)
    top = jnp.concatenate([wk3, z], axis=2)
    bot = jnp.concatenate([z, wv3], axis=2)
    return jnp.concatenate([top, bot], axis=1).reshape(CMP_STRIDE * 128, 128)


def _block_diag(mats):
    n = len(mats)
    rows = []
    for a, m in enumerate(mats):
        rows.append(jnp.concatenate([m if a == c else jnp.zeros_like(m) for c in range(n)], axis=1))
    return jnp.concatenate(rows, axis=0)


def _layer(x2, batch, seq, p):
    n = x2.shape[0]
    row = lambda v: v.reshape(1, -1)
    w_in = p["w_in"]
    w_in = jnp.concatenate(
        [w_in[:, :1280], w_in[:, 1676:1932], w_in[:, 1280:1676],
         jnp.zeros((D_MODEL, Z_COLS - 1932), w_in.dtype)], axis=1).astype(BF16)
    z, kvc = _inproj(x2, row(p["g_pre_mix"]), w_in)
    z3 = z.reshape(batch, seq, Z_COLS)

    wcat = jnp.transpose(p["sg_w"], (1, 0, 2)).reshape(SG_CHUNK, N_HEADS * SG_CHUNK)
    bias = jnp.repeat(p["sg_b"].T, HEAD_DIM, axis=1)
    y_a = _sgu(z3, row(p["sg_ln_g"]), wcat, bias)

    y_b = _conv(z3, p["cv_w"], row(p["cv_b"]), row(p["cv_ln_g"]), row(p["cv_ln_b"]),
                p["cv_pw"].astype(BF16), row(p["cv_pw_b"]))

    half = CMP_STRIDE
    pe = lambda a: jnp.concatenate([p["cmp_pos_k"][a * half:(a + 1) * half],
                                    p["cmp_pos_v"][a * half:(a + 1) * half]], axis=1).reshape(1, -1)
    wpart = lambda a: _interleave_rows(p["cmp_w1_k"][a * 1024:(a + 1) * 1024],
                                       p["cmp_w1_v"][a * 1024:(a + 1) * 1024]).astype(BF16)
    w2 = _block_diag([p["cmp_w2_k"], p["cmp_w2_v"]]).astype(BF16)
    kvc3 = kvc.reshape(batch, seq // CMP_STRIDE, CMP_STRIDE * 128)
    kcv = _compress(kvc3, pe(0), pe(1), wpart(0), wpart(1), w2)
    y_c = _nsa(z3, kcv)

    wpool = _block_diag([p["pool_w"][gi] for gi in range(len(POOL_WINDOWS))]).astype(BF16)
    y_d = _pool(z3, wpool, row(p["pool_scale"]))

    flat = lambda y: y.reshape(n, GW)
    x2 = _outproj(flat(y_a), flat(y_b), flat(y_c), flat(y_d), x2,
                  p["w_out"].astype(BF16), row(p["g_post_mix"]))
    wgu = p["ffn_w_gu"].astype(BF16)
    return _ffn(x2, row(p["g_pre_ffn"]), wgu[:, :FFN_HIDDEN], wgu[:, FFN_HIDDEN:],
                p["ffn_w_down"].astype(BF16), row(p["g_post_ffn"]))


_PARAM_NAMES = ("g_pre_mix", "g_post_mix", "g_pre_ffn", "g_post_ffn", "w_in", "sg_ln_g", "sg_w", "sg_b",
                "cv_w", "cv_b", "cv_ln_g", "cv_ln_b", "cv_pw", "cv_pw_b", "cmp_pos_k", "cmp_pos_v",
                "cmp_w1_k", "cmp_w2_k", "cmp_w1_v", "cmp_w2_v", "pool_w", "pool_scale", "w_out",
                "ffn_w_gu", "ffn_w_down")


def kernel(x, g_pre_mix, g_post_mix, g_pre_ffn, g_post_ffn, w_in, sg_ln_g, sg_w, sg_b, cv_w, cv_b, cv_ln_g, cv_ln_b, cv_pw, cv_pw_b, cmp_pos_k, cmp_pos_v, cmp_w1_k, cmp_w2_k, cmp_w1_v, cmp_w2_v, pool_w, pool_scale, w_out, ffn_w_gu, ffn_w_down):
    params = dict(zip(_PARAM_NAMES, (g_pre_mix, g_post_mix, g_pre_ffn, g_post_ffn, w_in, sg_ln_g, sg_w,
                                     sg_b, cv_w, cv_b, cv_ln_g, cv_ln_b, cv_pw, cv_pw_b, cmp_pos_k,
                                     cmp_pos_v, cmp_w1_k, cmp_w2_k, cmp_w1_v, cmp_w2_v, pool_w,
                                     pool_scale, w_out, ffn_w_gu, ffn_w_down)))
    batch, seq, _ = x.shape
    x2 = x.reshape(batch * seq, D_MODEL)
    for layer in range(g_pre_mix.shape[0]):
        x2 = _layer(x2, batch, seq, {k: v[layer] for k, v in params.items()})
    return x2.reshape(batch, seq, D_MODEL)
```

```python
import jax
import jax.numpy as jnp
from jax import lax
from jax.experimental import pallas as pl
from jax.experimental.pallas import tpu as pltpu

F32 = jnp.float32
BF16 = jnp.bfloat16

D_MODEL = 1024
GW = 256
HEAD_DIM = 64
N_HEADS = 4
SG_CHUNK = 128
CONV_WIDTH = 31
CMP_BLOCK = 32
CMP_STRIDE = 16
SLC_BLOCK = 64
SLC_TOPK = 8
WIN = 512
Q_BLOCK = 128
FORCE_BONUS = 1e4
NEG = -1e30
POOL_WINDOWS = (2, 4, 8, 16)
FFN_HIDDEN = 2816
RMS_EPS = 1e-6
LN_EPS = 1e-5
Z_COLS = 2048
Q_OFF = 1024
D_OFF = 1280
KV_OFF = 1536
KV_COLS = 512
N_CMP_PAD = 128
N_SLC = 32

TM = 512
TS = 512
CONV_HALO = 32
POOL_HALO = 16
SUB = 64
SUBLANES = 8
POOL_PAD = SUBLANES
SHIFT_CHUNK = 128
LOG2E = 1.4426950408889634
SLC_TK = 512
AUG = 128
PAD_LANE = HEAD_DIM + N_SLC
FFN_CHUNK = 1408
CMP_TB = 4
VMEM_LIMIT = 56 * 1024 * 1024


def _cparams(sem):
    return pltpu.CompilerParams(dimension_semantics=sem, vmem_limit_bytes=VMEM_LIMIT)


def _nt(a, b):
    return lax.dot_general(a, b, (((1,), (1,)), ((), ())), preferred_element_type=F32)


def _dot(a, b):
    return jnp.dot(a, b, preferred_element_type=F32)


def _rms(x, g):
    return (x * lax.rsqrt(jnp.mean(x * x, axis=-1, keepdims=True) + RMS_EPS)) * g


def _layernorm(x, g):
    mu = jnp.mean(x, axis=-1, keepdims=True)
    d = x - mu
    var = jnp.mean(d * d, axis=-1, keepdims=True)
    return (d * lax.rsqrt(var + LN_EPS)) * g


def _sgu_chunk(c, z_a, lng, w, bias, o_ref):
    lane_head = lax.broadcasted_iota(jnp.int32, (SG_CHUNK, GW), 1) // HEAD_DIM
    blk = z_a[c * SG_CHUNK:(c + 1) * SG_CHUNK, :]
    u = blk[:, :GW]
    vn = _layernorm(blk[:, GW:], lng).astype(BF16)
    zero = jnp.zeros_like(vn)
    v4 = jnp.concatenate([jnp.where(lane_head == h, vn, zero) for h in range(N_HEADS)], axis=0)
    sv = _dot(w, v4) + bias
    o_ref[0, c * SG_CHUNK:(c + 1) * SG_CHUNK, :] = (u * sv).astype(BF16)


def _conv_prep(si, z_b, hbuf):
    prev = hbuf[0, TS:TS + CONV_HALO, :]
    hbuf[0, 0:CONV_HALO, :] = jnp.where(si > 0, prev, 0.0)
    for r0 in range(0, TS, SUB):
        blk = z_b[r0:r0 + SUB, :]
        hbuf[0, CONV_HALO + r0:CONV_HALO + r0 + SUB, :] = blk[:, :GW] * jax.nn.sigmoid(blk[:, GW:])
    rows = CONV_HALO + TS - SUBLANES
    for r in range(1, SUBLANES):
        for j0 in range(0, rows, SHIFT_CHUNK):
            n = min(SHIFT_CHUNK, rows - j0)
            hbuf[r, j0:j0 + n, :] = hbuf[0, j0 + r:j0 + r + n, :]


def _conv_chunk(r0, cw, cb, lng, lnb, pw, pwb, o_ref, hbuf):
    lead = CONV_HALO - (CONV_WIDTH - 1)
    acc = jnp.zeros((SUB, GW), F32)
    for k in range(CONV_WIDTH):
        off = lead + k
        base = r0 + off - off % SUBLANES
        acc = acc + cw[k:k + 1, :] * hbuf[off % SUBLANES, base:base + SUB, :]
    y = _layernorm(acc + cb, lng) + lnb
    y = y * jax.nn.sigmoid(y)
    out = _dot(y.astype(BF16), pw) + pwb
    o_ref[0, r0:r0 + SUB, :] = out.astype(BF16)


def _pool_prep(si, z_d, xbuf, s2buf, s4buf, s8buf):
    data0 = POOL_PAD + POOL_HALO
    total = data0 + TS
    zeros = jnp.zeros((POOL_PAD, GW), F32)
    prev = xbuf[total - POOL_HALO:total, :]
    xbuf[0:POOL_PAD, :] = zeros
    s2buf[0:POOL_PAD, :] = zeros
    s4buf[0:POOL_PAD, :] = zeros
    xbuf[POOL_PAD:data0, :] = jnp.where(si > 0, prev, 0.0)
    xbuf[data0:total, :] = z_d
    for src, dst, shift in ((xbuf, s2buf, 1), (s2buf, s4buf, 2), (s4buf, s8buf, 4)):
        for j0 in range(POOL_PAD, total, SHIFT_CHUNK):
            n = min(SHIFT_CHUNK, total - j0)
            dst[j0:j0 + n, :] = src[j0:j0 + n, :] + src[j0 - shift:j0 - shift + n, :]


def _pool_chunk(si, r0, w, scale, o_ref, xbuf, s2buf, s4buf, s8buf):
    lane_grp = lax.broadcasted_iota(jnp.int32, (1, GW), 1) // (GW // len(POOL_WINDOWS))
    win = jnp.zeros((1, GW), jnp.int32)
    for gi, wlen in enumerate(POOL_WINDOWS):
        win = jnp.where(lane_grp == gi, wlen, win)
    j = POOL_PAD + POOL_HALO + r0
    x = xbuf[j:j + SUB, :]
    s8 = s8buf[j:j + SUB, :]
    s16 = s8 + s8buf[j - 8:j - 8 + SUB, :]
    acc = jnp.where(lane_grp == 0, s2buf[j:j + SUB, :],
                    jnp.where(lane_grp == 1, s4buf[j:j + SUB, :],
                              jnp.where(lane_grp == 2, s8, s16)))
    t1 = si * TS + r0 + 1 + lax.broadcasted_iota(jnp.int32, (SUB, GW), 0)
    cnt = jnp.minimum(t1, win).astype(F32)
    diff = acc / cnt - x
    out = _dot(diff.astype(BF16), w) * scale
    o_ref[0, r0:r0 + SUB, :] = out.astype(BF16)


def _front_kernel(x_ref, g_ref, w_ref,
                  sg_lng_ref, sg_w_ref, sg_bias_ref,
                  cw_ref, cb_ref, cv_lng_ref, cv_lnb_ref, pw_ref, pwb_ref,
                  pool_w_ref, pool_scale_ref,
                  ya_ref, yb_ref, yd_ref, zq_ref, zkv_ref,
                  hbuf, xbuf, s2buf, s4buf, s8buf):
    si = pl.program_id(1)

    @pl.when(si == 0)
    def _():
        hbuf[0, TS:TS + CONV_HALO, :] = jnp.zeros((CONV_HALO, GW), F32)
        xbuf[POOL_PAD + TS:POOL_PAD + POOL_HALO + TS, :] = jnp.zeros((POOL_HALO, GW), F32)

    conv_args = (cw_ref[...], cb_ref[...], cv_lng_ref[...], cv_lnb_ref[...], pw_ref[...], pwb_ref[...],
                 yb_ref, hbuf)
    pool_bufs = (xbuf, s2buf, s4buf, s8buf)
    conv = lambda k: _conv_chunk(k * SUB, *conv_args)
    pool = lambda k: _pool_chunk(si, k * SUB, pool_w_ref[...], pool_scale_ref[...], yd_ref, *pool_bufs)

    h = _rms(x_ref[0], g_ref[...]).astype(BF16)
    _conv_prep(si, _dot(h, w_ref[:, 2 * GW:4 * GW]), hbuf)
    z_d = _dot(h, w_ref[:, D_OFF:D_OFF + GW])
    conv(0)
    conv(1)
    _pool_prep(si, z_d, *pool_bufs)
    z_a = _dot(h, w_ref[:, 0:2 * GW])
    conv(2)
    conv(3)
    pool(0)
    pool(1)
    zq_ref[0] = _dot(h, w_ref[:, Q_OFF:Q_OFF + GW])
    conv(4)
    conv(5)
    pool(2)
    pool(3)
    zkv_ref[0, :, 0:GW] = _dot(h, w_ref[:, KV_OFF:KV_OFF + GW])
    conv(6)
    conv(7)
    pool(4)
    pool(5)
    zkv_ref[0, :, GW:KV_COLS] = _dot(h, w_ref[:, KV_OFF + GW:KV_OFF + KV_COLS])
    pool(6)
    pool(7)
    row = lax.broadcasted_iota(jnp.int32, (SG_CHUNK, 4 * SG_CHUNK), 0)
    col = lax.broadcasted_iota(jnp.int32, (SG_CHUNK, 4 * SG_CHUNK), 1) % SG_CHUNK
    sg_w = jnp.where(row >= col, sg_w_ref[...], 0.0).astype(BF16)
    for c in range(TS // SG_CHUNK):
        _sgu_chunk(c, z_a, sg_lng_ref[...], sg_w, sg_bias_ref[...], ya_ref)


def _front(x3, g, w, sg_lng, sg_w, sg_bias, cw, cb, cv_lng, cv_lnb, pw, pwb, pool_w, pool_scale):
    b, s, _ = x3.shape
    const = lambda shape: pl.BlockSpec(shape, lambda bi, si: (0,) * len(shape))
    tile = lambda width: pl.BlockSpec((1, TS, width), lambda bi, si: (bi, si, 0))
    pool_rows = POOL_PAD + POOL_HALO + TS
    return pl.pallas_call(
        _front_kernel,
        grid=(b, s // TS),
        in_specs=[
            tile(D_MODEL), const((1, D_MODEL)), const((D_MODEL, Z_COLS)),
            const((1, GW)), const((SG_CHUNK, 4 * SG_CHUNK)), const((SG_CHUNK, GW)),
            const((CONV_WIDTH, GW)), const((1, GW)), const((1, GW)), const((1, GW)),
            const((GW, GW)), const((1, GW)),
            const((GW, GW)), const((1, GW)),
        ],
        out_specs=[tile(GW), tile(GW), tile(GW), tile(GW), tile(KV_COLS)],
        out_shape=[
            jax.ShapeDtypeStruct((b, s, GW), BF16),
            jax.ShapeDtypeStruct((b, s, GW), BF16),
            jax.ShapeDtypeStruct((b, s, GW), BF16),
            jax.ShapeDtypeStruct((b, s, GW), F32),
            jax.ShapeDtypeStruct((b, s, KV_COLS), F32),
        ],
        scratch_shapes=[pltpu.VMEM((SUBLANES, CONV_HALO + TS, GW), F32)]
        + [pltpu.VMEM((pool_rows, GW), F32)] * 4,
        compiler_params=_cparams(("parallel", "arbitrary")),
        name="front",
    )(x3, g, w, sg_lng, sg_w, sg_bias, cw, cb, cv_lng, cv_lnb, pw, pwb, pool_w, pool_scale)


def _compress_kernel(x_ref, pe_ref, w1_ref, w2_ref, o_ref):
    m = CMP_TB * N_CMP_PAD
    p0 = jnp.zeros((m, 128), F32)
    p1 = jnp.zeros((m, 128), F32)
    for r in range(CMP_STRIDE):
        xr = jnp.concatenate([x_ref[bi, pl.ds(r, N_CMP_PAD, stride=CMP_STRIDE), :] for bi in range(CMP_TB)],
                             axis=0)
        p0 = p0 + _dot((xr + pe_ref[r:r + 1, :]).astype(BF16), w1_ref[r])
        p1 = p1 + _dot((xr + pe_ref[CMP_STRIDE + r:CMP_STRIDE + r + 1, :]).astype(BF16),
                       w1_ref[CMP_STRIDE + r])
    pre = p0 + pltpu.roll(p1, m - 1, axis=0)
    hid = pre * jax.nn.sigmoid(pre)
    out = _dot(hid.astype(BF16), w2_ref[...])
    rowid = lax.broadcasted_iota(jnp.int32, (m, 128), 0) % N_CMP_PAD
    out = jnp.where(rowid < N_CMP_PAD - 1, out, 0.0)
    o_ref[...] = out.reshape(CMP_TB, N_CMP_PAD, 128)


def _compress(zkv, pe, w1, w2):
    b, s, _ = zkv.shape
    return pl.pallas_call(
        _compress_kernel,
        grid=(b // CMP_TB,),
        in_specs=[
            pl.BlockSpec((CMP_TB, s, 128), lambda i: (i, 0, 0)),
            pl.BlockSpec((CMP_BLOCK, 128), lambda i: (0, 0)),
            pl.BlockSpec((CMP_BLOCK, 128, 128), lambda i: (0, 0, 0)),
            pl.BlockSpec((128, 128), lambda i: (0, 0)),
        ],
        out_specs=pl.BlockSpec((CMP_TB, N_CMP_PAD, 128), lambda i: (i, 0, 0)),
        out_shape=jax.ShapeDtypeStruct((b, N_CMP_PAD, 128), F32),
        compiler_params=_cparams(("parallel",)),
        name="nsa_compress",
    )(zkv, pe, w1, w2)


def _nsa_kernel(q_ref, kv_ref, kcv_ref, o_ref, ks_aug, kts, vs_aug, kw_aug, vw_aug, gsel):
    i = pl.program_id(1)
    seq = kv_ref.shape[1]
    nq = Q_BLOCK

    @pl.when(i == 0)
    def _():
        eye = jnp.where(lax.broadcasted_iota(jnp.int32, (AUG, AUG), 0)
                        == lax.broadcasted_iota(jnp.int32, (AUG, AUG), 1), 1.0, 0.0).astype(BF16)
        lane_p = lax.broadcasted_iota(jnp.int32, (WIN, AUG), 1)
        kw_aug[0:WIN, :] = jnp.where(lane_p == PAD_LANE, 1.0, 0.0).astype(BF16)
        vw_aug[0:WIN, :] = jnp.zeros((WIN, 2 * AUG), BF16)
        gsel[...] = jnp.where(lax.broadcasted_iota(jnp.int32, gsel.shape, 0)
                              == lax.broadcasted_iota(jnp.int32, gsel.shape, 1) // AUG, 1.0, 0.0).astype(BF16)
        lane = lax.broadcasted_iota(jnp.int32, (256, AUG), 1)
        ones = jnp.ones((256, AUG), BF16)
        for r in range(0, seq, 256):
            blk = kv_ref[0, r:r + 256, :]
            t_s = blk[:, 128:256]
            t_w = blk[:, 256:384]
            key_blk = (r + lax.broadcasted_iota(jnp.int32, (256, AUG), 0)) // SLC_BLOCK
            onehot = jnp.where(lane - HEAD_DIM == key_blk, 1.0, 0.0)
            ks_aug[r:r + 256, :] = jnp.where(lane < HEAD_DIM, t_s, onehot).astype(BF16)
            v_s = jnp.where(lane < HEAD_DIM, pltpu.roll(t_s, HEAD_DIM, axis=1), 0.0).astype(BF16)
            vs_aug[r:r + 256, :] = jnp.concatenate([v_s, ones], axis=1)
            kw_aug[WIN + r:WIN + r + 256, :] = jnp.where(lane < HEAD_DIM, t_w, 0.0).astype(BF16)
            v_w = jnp.where(lane < HEAD_DIM, pltpu.roll(t_w, HEAD_DIM, axis=1), 0.0).astype(BF16)
            vw_aug[WIN + r:WIN + r + 256, :] = jnp.concatenate([v_w, ones], axis=1)
        for c in range(0, seq, SLC_TK):
            kts[:, c:c + SLC_TK] = _nt(eye, ks_aug[c:c + SLC_TK, :]).astype(BF16)

    per = SLC_TK // nq
    n_dyn = (i + per - 1) // per
    for n_kt in range(seq // SLC_TK + 1):
        @pl.when(n_dyn == n_kt)
        def _(n_kt=n_kt):
            _nsa_tile(n_kt, i, q_ref, kv_ref, kcv_ref, o_ref, ks_aug, kts, vs_aug, kw_aug, vw_aug, gsel)


def _nsa_tile(n_kt, i, q_ref, kv_ref, kcv_ref, o_ref, ks_aug, kts, vs_aug, kw_aug, vw_aug, gsel):
    nq = Q_BLOCK
    hq = N_HEADS * nq
    eye = jnp.where(lax.broadcasted_iota(jnp.int32, (AUG, AUG), 0)
                    == lax.broadcasted_iota(jnp.int32, (AUG, AUG), 1), 1.0, 0.0).astype(BF16)
    q0 = pl.multiple_of(i * nq, nq)
    q = q_ref[0] * (HEAD_DIM ** -0.5 * LOG2E)
    lane_q = lax.broadcasted_iota(jnp.int32, (nq, AUG), 1)

    def stack_heads(extra):
        tiles = []
        for h in range(N_HEADS):
            t = q[:, AUG * (h // 2):AUG * (h // 2 + 1)]
            if h % 2:
                t = pltpu.roll(t, HEAD_DIM, axis=1)
            tiles.append(jnp.where(lane_q < HEAD_DIM, t, extra))
        return jnp.concatenate(tiles, axis=0).astype(BF16)

    pad_bias = jnp.where(lane_q == PAD_LANE, NEG, 0.0)
    qw = stack_heads(pad_bias)
    trow = q0 + lax.broadcasted_iota(jnp.int32, (nq, 1), 0)
    row_l = lax.broadcasted_iota(jnp.int32, (nq, nq), 0)
    col_l = lax.broadcasted_iota(jnp.int32, (nq, nq), 1)
    tri_le = (col_l <= row_l)[None]
    tri_gt = (col_l > row_l)[None]

    nk = WIN + nq
    sw = _nt(qw, kw_aug[pl.ds(q0, nk), :]).reshape(N_HEADS, nq, nk)
    sw = jnp.concatenate([jnp.where(tri_gt, sw[:, :, :nq], NEG), sw[:, :, nq:WIN],
                          jnp.where(tri_le, sw[:, :, WIN:], NEG)], axis=-1)
    pw = jnp.exp2(sw - jnp.max(sw, axis=-1, keepdims=True)).astype(BF16)
    o_win = _dot(pw.reshape(hq, nk), vw_aug[pl.ds(q0, nk), :])

    sd = _nt(qw, ks_aug[pl.ds(q0, nq), :]).reshape(N_HEADS, nq, nq)
    sd = jnp.where(tri_le, sd, NEG).reshape(hq, nq)
    md = jnp.max(sd, axis=-1, keepdims=True)

    kcv = kcv_ref[0]
    lane_c = lax.broadcasted_iota(jnp.int32, (N_CMP_PAD, AUG), 1)
    kc = jnp.where(lane_c < HEAD_DIM, kcv, 0.0).astype(BF16)
    vc = jnp.where(lane_c < HEAD_DIM, pltpu.roll(kcv, HEAD_DIM, axis=1), 0.0).astype(BF16)
    s = _nt(qw, kc).reshape(N_HEADS, nq, N_CMP_PAD)
    cidx = lax.broadcasted_iota(jnp.int32, (nq, N_CMP_PAD), 1)
    cmask = (cidx * CMP_STRIDE + (CMP_BLOCK - 1) <= trow) & (cidx < N_CMP_PAD - 1)
    sm = jnp.where(cmask[None], s, NEG)
    e = jnp.exp2(sm - jnp.max(sm, axis=-1, keepdims=True))
    p = e / jnp.sum(e, axis=-1, keepdims=True)
    p = jnp.where((trow >= CMP_BLOCK - 1)[None], p, 0.0)
    o_cmp = _dot(p.reshape(hq, N_CMP_PAD).astype(BF16), vc)

    psum = jnp.sum(p, axis=0)
    p_hi = psum.astype(BF16)
    p_lo = (psum - p_hi.astype(F32)).astype(BF16)
    jrow = lax.broadcasted_iota(jnp.int32, (N_SLC, N_CMP_PAD), 0)
    ccol = lax.broadcasted_iota(jnp.int32, (N_SLC, N_CMP_PAD), 1)
    ov = ((ccol * CMP_STRIDE <= jrow * SLC_BLOCK + (SLC_BLOCK - 1))
          & (ccol * CMP_STRIDE + (CMP_BLOCK - 1) >= jrow * SLC_BLOCK)
          & (ccol < N_CMP_PAD - 1))
    ov = jnp.where(ov, 1.0, 0.0).astype(BF16)
    imp = _nt(ov, p_hi) + _nt(ov, p_lo)
    jj = lax.broadcasted_iota(jnp.int32, (N_SLC, nq), 0)
    tt = q0 + lax.broadcasted_iota(jnp.int32, (N_SLC, nq), 1)
    cur = tt // SLC_BLOCK
    valid = jj * SLC_BLOCK <= tt
    forced = (jj == 0) | (jj == cur) | (jj == cur - 1)
    score = jnp.where(valid, imp + jnp.where(forced, FORCE_BONUS, 0.0), NEG)
    rank = jnp.zeros((N_SLC, nq), F32)
    for j2 in range(N_SLC):
        other = score[j2:j2 + 1, :]
        beats = (other > score) | ((other == score) & (jj > j2))
        rank = rank + jnp.where(beats, 1.0, 0.0)
    sel_t = jnp.where((rank < SLC_TOPK) & (score > NEG / 2), 1.0, 0.0)
    bias_t = jnp.where((sel_t > 0.5) & (jj < 2 * i), 0.0, NEG)
    bias_t = jnp.concatenate([jnp.zeros((HEAD_DIM, nq), F32), bias_t,
                              jnp.zeros((AUG - HEAD_DIM - N_SLC, nq), F32)], axis=0).astype(BF16)
    blk_bias = _nt(eye, bias_t)

    if n_kt:
        n_main = n_kt * SLC_TK
        qs = stack_heads(blk_bias + pad_bias)
        s_main = _dot(qs, kts[:, 0:n_main])
        m = jnp.maximum(md, jnp.max(s_main, axis=-1, keepdims=True))
        p_main = jnp.exp2(s_main - m).astype(BF16)
        p_diag = jnp.exp2(sd - m).astype(BF16)
        o_slc = _dot(p_diag, vs_aug[pl.ds(q0, nq), :]) + _dot(p_main, vs_aug[0:n_main, :])
    else:
        o_slc = _dot(jnp.exp2(sd - md).astype(BF16), vs_aug[pl.ds(q0, nq), :])

    g = jax.nn.sigmoid(kv_ref[0, pl.ds(q0, nq), 384:384 + AUG])
    g_hi = g.astype(BF16)
    g_lo = (g - g_hi.astype(F32)).astype(BF16)
    g_rep = _dot(g_hi, gsel[...]) + _dot(g_lo, gsel[...])
    gate = lambda br: jnp.concatenate(
        [g_rep[:, (3 * h + br) * AUG:(3 * h + br + 1) * AUG] for h in range(N_HEADS)], axis=0)
    o = (gate(0) * o_cmp
         + (gate(1) / o_slc[:, AUG:]) * o_slc[:, :AUG]
         + (gate(2) / o_win[:, AUG:]) * o_win[:, :AUG])
    halves = [jnp.where(lane_q < HEAD_DIM, o[2 * a * nq:(2 * a + 1) * nq, :],
                        pltpu.roll(o[(2 * a + 1) * nq:(2 * a + 2) * nq, :], HEAD_DIM, axis=1))
              for a in range(N_HEADS // 2)]
    o_ref[0] = jnp.concatenate(halves, axis=1).astype(BF16)


def _nsa(zq, zkv, kcv):
    b, s, _ = zq.shape
    return pl.pallas_call(
        _nsa_kernel,
        grid=(b, s // Q_BLOCK),
        in_specs=[
            pl.BlockSpec((1, Q_BLOCK, GW), lambda bi, qi: (bi, qi, 0)),
            pl.BlockSpec((1, s, KV_COLS), lambda bi, qi: (bi, 0, 0)),
            pl.BlockSpec((1, N_CMP_PAD, 128), lambda bi, qi: (bi, 0, 0)),
        ],
        out_specs=pl.BlockSpec((1, Q_BLOCK, GW), lambda bi, qi: (bi, qi, 0)),
        out_shape=jax.ShapeDtypeStruct((b, s, GW), BF16),
        scratch_shapes=[
            pltpu.VMEM((s, AUG), BF16),
            pltpu.VMEM((AUG, s), BF16),
            pltpu.VMEM((s, 2 * AUG), BF16),
            pltpu.VMEM((WIN + s, AUG), BF16),
            pltpu.VMEM((WIN + s, 2 * AUG), BF16),
            pltpu.VMEM((AUG, 3 * N_HEADS * AUG), BF16),
        ],
        compiler_params=_cparams(("parallel", "arbitrary")),
        name="nsa_attn",
    )(zq, zkv, kcv)


def _back_kernel(ya_ref, yb_ref, yc_ref, yd_ref, x_ref, wo_ref, gmix_ref, gpre_ref, wg_ref, wu_ref, wd_ref,
                 gpost_ref, o_ref):
    y = jnp.concatenate([ya_ref[...], yb_ref[...], yc_ref[...], yd_ref[...]], axis=1)
    x = x_ref[...] + _rms(_dot(y, wo_ref[...]), gmix_ref[...])
    h = _rms(x, gpre_ref[...]).astype(BF16)
    f = jnp.zeros((TM, D_MODEL), F32)
    for c0 in range(0, FFN_HIDDEN, FFN_CHUNK):
        gate = _dot(h, wg_ref[:, c0:c0 + FFN_CHUNK])
        up = _dot(h, wu_ref[:, c0:c0 + FFN_CHUNK])
        act = ((gate * jax.nn.sigmoid(gate)) * up).astype(BF16)
        f = f + _dot(act, wd_ref[c0:c0 + FFN_CHUNK, :])
    o_ref[...] = x + _rms(f, gpost_ref[...])


def _back(ya, yb, yc, yd, x2, wo, gmix, gpre, wg, wu, wd, gpost):
    n = x2.shape[0]
    const = lambda shape: pl.BlockSpec(shape, lambda i: (0, 0), pipeline_mode=pl.Buffered(1))
    yspec = pl.BlockSpec((TM, GW), lambda i: (i, 0))
    return pl.pallas_call(
        _back_kernel,
        grid=(n // TM,),
        in_specs=[
            yspec, yspec, yspec, yspec,
            pl.BlockSpec((TM, D_MODEL), lambda i: (i, 0)),
            const((D_MODEL, D_MODEL)), const((1, D_MODEL)), const((1, D_MODEL)),
            const((D_MODEL, FFN_HIDDEN)), const((D_MODEL, FFN_HIDDEN)), const((FFN_HIDDEN, D_MODEL)),
            const((1, D_MODEL)),
        ],
        out_specs=pl.BlockSpec((TM, D_MODEL), lambda i: (i, 0)),
        out_shape=jax.ShapeDtypeStruct((n, D_MODEL), F32),
        compiler_params=_cparams(("parallel",)),
        name="back",
    )(ya, yb, yc, yd, x2, wo, gmix, gpre, wg, wu, wd, gpost)


def _interleave(wk, wv):
    z = jnp.zeros((CMP_BLOCK, HEAD_DIM, HEAD_DIM), wk.dtype)
    wk3 = wk.reshape(CMP_BLOCK, HEAD_DIM, HEAD_DIM)
    wv3 = wv.reshape(CMP_BLOCK, HEAD_DIM, HEAD_DIM)
    return jnp.concatenate([jnp.concatenate([wk3, z], axis=2), jnp.concatenate([z, wv3], axis=2)], axis=1)


def _block_diag(mats):
    n = len(mats)
    rows = []
    for a, m in enumerate(mats):
        rows.append(jnp.concatenate([m if a == c else jnp.zeros_like(m) for c in range(n)], axis=1))
    return jnp.concatenate(rows, axis=0)


def _layer(x2, batch, seq, p):
    n = x2.shape[0]
    row = lambda v: v.reshape(1, -1)
    w_in = p["w_in"]
    w_in = jnp.concatenate(
        [w_in[:, :1280], w_in[:, 1676:1932], w_in[:, 1280:1676],
         jnp.zeros((D_MODEL, Z_COLS - 1932), w_in.dtype)], axis=1).astype(BF16)
    sg_w = jnp.transpose(p["sg_w"], (1, 0, 2)).reshape(SG_CHUNK, N_HEADS * SG_CHUNK)
    sg_bias = jnp.repeat(p["sg_b"].T, HEAD_DIM, axis=1)
    pool_w = _block_diag([p["pool_w"][gi] for gi in range(len(POOL_WINDOWS))]).astype(BF16)
    y_a, y_b, y_d, zq, zkv = _front(
        x2.reshape(batch, seq, D_MODEL), row(p["g_pre_mix"]), w_in,
        row(p["sg_ln_g"]), sg_w, sg_bias,
        p["cv_w"], row(p["cv_b"]), row(p["cv_ln_g"]), row(p["cv_ln_b"]), p["cv_pw"].astype(BF16),
        row(p["cv_pw_b"]), pool_w, row(p["pool_scale"]))

    pe = jnp.concatenate([p["cmp_pos_k"], p["cmp_pos_v"]], axis=1)
    w1 = _interleave(p["cmp_w1_k"], p["cmp_w1_v"]).astype(BF16)
    w2 = _block_diag([p["cmp_w2_k"], p["cmp_w2_v"]]).astype(BF16)
    kcv = _compress(zkv, pe, w1, w2)
    y_c = _nsa(zq, zkv, kcv)

    flat = lambda y: y.reshape(n, GW)
    wgu = p["ffn_w_gu"].astype(BF16)
    return _back(flat(y_a), flat(y_b), flat(y_c), flat(y_d), x2,
                 p["w_out"].astype(BF16), row(p["g_post_mix"]), row(p["g_pre_ffn"]),
                 wgu[:, :FFN_HIDDEN], wgu[:, FFN_HIDDEN:], p["ffn_w_down"].astype(BF16),
                 row(p["g_post_ffn"]))


_PARAM_NAMES = ("g_pre_mix", "g_post_mix", "g_pre_ffn", "g_post_ffn", "w_in", "sg_ln_g", "sg_w", "sg_b",
                "cv_w", "cv_b", "cv_ln_g", "cv_ln_b", "cv_pw", "cv_pw_b", "cmp_pos_k", "cmp_pos_v",
                "cmp_w1_k", "cmp_w2_k", "cmp_w1_v", "cmp_w2_v", "pool_w", "pool_scale", "w_out",
                "ffn_w_gu", "ffn_w_down")


def kernel(x, g_pre_mix, g_post_mix, g_pre_ffn, g_post_ffn, w_in, sg_ln_g, sg_w, sg_b, cv_w, cv_b, cv_ln_g, cv_ln_b, cv_pw, cv_pw_b, cmp_pos_k, cmp_pos_v, cmp_w1_k, cmp_w2_k, cmp_w1_v, cmp_w2_v, pool_w, pool_scale, w_out, ffn_w_gu, ffn_w_down):
    params = dict(zip(_PARAM_NAMES, (g_pre_mix, g_post_mix, g_pre_ffn, g_post_ffn, w_in, sg_ln_g, sg_w,
                                     sg_b, cv_w, cv_b, cv_ln_g, cv_ln_b, cv_pw, cv_pw_b, cmp_pos_k,
                                     cmp_pos_v, cmp_w1_k, cmp_w2_k, cmp_w1_v, cmp_w2_v, pool_w,
                                     pool_scale, w_out, ffn_w_gu, ffn_w_down)))
    batch, seq, _ = x.shape
    x2 = x.reshape(batch * seq, D_MODEL)
    for layer in range(g_pre_mix.shape[0]):
        x2 = _layer(x2, batch, seq, {k: v[layer] for k, v in params.items()})
    return x2.reshape(batch, seq, D_MODEL)
```

```python
import jax
import jax.numpy as jnp
from jax import lax
from jax.experimental import pallas as pl
from jax.experimental.pallas import tpu as pltpu

F32 = jnp.float32
BF16 = jnp.bfloat16

D_MODEL = 1024
GW = 256
HEAD_DIM = 64
N_HEADS = 4
SG_CHUNK = 128
CONV_WIDTH = 31
CMP_BLOCK = 32
CMP_STRIDE = 16
SLC_BLOCK = 64
SLC_TOPK = 8
WIN = 512
Q_BLOCK = 128
FORCE_BONUS = 1e4
NEG = -1e30
POOL_WINDOWS = (2, 4, 8, 16)
FFN_HIDDEN = 2816
RMS_EPS = 1e-6
LN_EPS = 1e-5
Z_COLS = 2048
Q_OFF = 1024
D_OFF = 1280
KV_OFF = 1536
KV_COLS = 512
N_CMP_PAD = 128
N_SLC = 32

TM = 512
TS = 512
CONV_HALO = 32
POOL_HALO = 16
SUB = 64
SUBLANES = 8
POOL_PAD = SUBLANES
SHIFT_CHUNK = 128
LOG2E = 1.4426950408889634
SLC_TK = 512
AUG = 128
PAD_LANE = HEAD_DIM + N_SLC
MXU_DIM = 256
FFN_CHUNKS = (6 * MXU_DIM, 5 * MXU_DIM)
CMP_TB = 4
VMEM_LIMIT = 56 * 1024 * 1024


def _cparams(sem):
    return pltpu.CompilerParams(dimension_semantics=sem, vmem_limit_bytes=VMEM_LIMIT)


def _nt(a, b):
    return lax.dot_general(a, b, (((1,), (1,)), ((), ())), preferred_element_type=F32)


def _dot(a, b):
    return jnp.dot(a, b, preferred_element_type=F32)


def _rms(x, g):
    return (x * lax.rsqrt(jnp.mean(x * x, axis=-1, keepdims=True) + RMS_EPS)) * g


def _layernorm(x, g):
    mu = jnp.mean(x, axis=-1, keepdims=True)
    d = x - mu
    var = jnp.mean(d * d, axis=-1, keepdims=True)
    return (d * lax.rsqrt(var + LN_EPS)) * g


def _sgu_chunk(c, z_a, lng, w, bias, o_ref):
    lane_head = lax.broadcasted_iota(jnp.int32, (SG_CHUNK, GW), 1) // HEAD_DIM
    blk = z_a[c * SG_CHUNK:(c + 1) * SG_CHUNK, :]
    u = blk[:, :GW]
    vn = _layernorm(blk[:, GW:], lng).astype(BF16)
    zero = jnp.zeros_like(vn)
    v4 = jnp.concatenate([jnp.where(lane_head == h, vn, zero) for h in range(N_HEADS)], axis=0)
    sv = _dot(w, v4) + bias
    o_ref[0, c * SG_CHUNK:(c + 1) * SG_CHUNK, :] = (u * sv).astype(BF16)


def _conv_prep(si, z_b, hbuf):
    prev = hbuf[0, TS:TS + CONV_HALO, :]
    hbuf[0, 0:CONV_HALO, :] = jnp.where(si > 0, prev, 0.0)
    for r0 in range(0, TS, SUB):
        blk = z_b[r0:r0 + SUB, :]
        hbuf[0, CONV_HALO + r0:CONV_HALO + r0 + SUB, :] = blk[:, :GW] * jax.nn.sigmoid(blk[:, GW:])
    rows = CONV_HALO + TS - SUBLANES
    for r in range(1, SUBLANES):
        for j0 in range(0, rows, SHIFT_CHUNK):
            n = min(SHIFT_CHUNK, rows - j0)
            hbuf[r, j0:j0 + n, :] = hbuf[0, j0 + r:j0 + r + n, :]


def _conv_chunk(r0, cw, cb, lng, lnb, pw, pwb, o_ref, hbuf):
    lead = CONV_HALO - (CONV_WIDTH - 1)
    acc = jnp.zeros((SUB, GW), F32)
    for k in range(CONV_WIDTH):
        off = lead + k
        base = r0 + off - off % SUBLANES
        acc = acc + cw[k:k + 1, :] * hbuf[off % SUBLANES, base:base + SUB, :]
    y = _layernorm(acc + cb, lng) + lnb
    y = y * jax.nn.sigmoid(y)
    out = _dot(y.astype(BF16), pw) + pwb
    o_ref[0, r0:r0 + SUB, :] = out.astype(BF16)


def _pool_prep(si, z_d, xbuf, s2buf, s4buf, s8buf):
    data0 = POOL_PAD + POOL_HALO
    total = data0 + TS
    zeros = jnp.zeros((POOL_PAD, GW), F32)
    prev = xbuf[total - POOL_HALO:total, :]
    xbuf[0:POOL_PAD, :] = zeros
    s2buf[0:POOL_PAD, :] = zeros
    s4buf[0:POOL_PAD, :] = zeros
    xbuf[POOL_PAD:data0, :] = jnp.where(si > 0, prev, 0.0)
    xbuf[data0:total, :] = z_d
    for src, dst, shift in ((xbuf, s2buf, 1), (s2buf, s4buf, 2), (s4buf, s8buf, 4)):
        for j0 in range(POOL_PAD, total, SHIFT_CHUNK):
            n = min(SHIFT_CHUNK, total - j0)
            dst[j0:j0 + n, :] = src[j0:j0 + n, :] + src[j0 - shift:j0 - shift + n, :]


def _pool_chunk(si, r0, w, scale, o_ref, xbuf, s2buf, s4buf, s8buf):
    lane_grp = lax.broadcasted_iota(jnp.int32, (1, GW), 1) // (GW // len(POOL_WINDOWS))
    win = jnp.zeros((1, GW), jnp.int32)
    for gi, wlen in enumerate(POOL_WINDOWS):
        win = jnp.where(lane_grp == gi, wlen, win)
    j = POOL_PAD + POOL_HALO + r0
    x = xbuf[j:j + SUB, :]
    s8 = s8buf[j:j + SUB, :]
    s16 = s8 + s8buf[j - 8:j - 8 + SUB, :]
    acc = jnp.where(lane_grp == 0, s2buf[j:j + SUB, :],
                    jnp.where(lane_grp == 1, s4buf[j:j + SUB, :],
                              jnp.where(lane_grp == 2, s8, s16)))
    t1 = si * TS + r0 + 1 + lax.broadcasted_iota(jnp.int32, (SUB, GW), 0)
    cnt = jnp.minimum(t1, win).astype(F32)
    diff = acc / cnt - x
    out = _dot(diff.astype(BF16), w) * scale
    o_ref[0, r0:r0 + SUB, :] = out.astype(BF16)


def _front_kernel(x_ref, g_ref, w_ref,
                  sg_lng_ref, sg_w_ref, sg_bias_ref,
                  cw_ref, cb_ref, cv_lng_ref, cv_lnb_ref, pw_ref, pwb_ref,
                  pool_w_ref, pool_scale_ref,
                  ya_ref, yb_ref, yd_ref, zq_ref, zkv_ref,
                  hbuf, xbuf, s2buf, s4buf, s8buf):
    si = pl.program_id(1)

    @pl.when(si == 0)
    def _():
        hbuf[0, TS:TS + CONV_HALO, :] = jnp.zeros((CONV_HALO, GW), F32)
        xbuf[POOL_PAD + TS:POOL_PAD + POOL_HALO + TS, :] = jnp.zeros((POOL_HALO, GW), F32)

    conv_args = (cw_ref[...], cb_ref[...], cv_lng_ref[...], cv_lnb_ref[...], pw_ref[...], pwb_ref[...],
                 yb_ref, hbuf)
    pool_bufs = (xbuf, s2buf, s4buf, s8buf)
    conv = lambda k: _conv_chunk(k * SUB, *conv_args)
    pool = lambda k: _pool_chunk(si, k * SUB, pool_w_ref[...], pool_scale_ref[...], yd_ref, *pool_bufs)

    h = _rms(x_ref[0], g_ref[...]).astype(BF16)
    _conv_prep(si, _dot(h, w_ref[:, 2 * GW:4 * GW]), hbuf)
    z_d = _dot(h, w_ref[:, D_OFF:D_OFF + GW])
    conv(0)
    conv(1)
    _pool_prep(si, z_d, *pool_bufs)
    z_a = _dot(h, w_ref[:, 0:2 * GW])
    conv(2)
    conv(3)
    pool(0)
    pool(1)
    zq_ref[0] = _dot(h, w_ref[:, Q_OFF:Q_OFF + GW])
    conv(4)
    conv(5)
    pool(2)
    pool(3)
    zkv_ref[0, :, 0:GW] = _dot(h, w_ref[:, KV_OFF:KV_OFF + GW])
    conv(6)
    conv(7)
    pool(4)
    pool(5)
    zkv_ref[0, :, GW:KV_COLS] = _dot(h, w_ref[:, KV_OFF + GW:KV_OFF + KV_COLS])
    pool(6)
    pool(7)
    row = lax.broadcasted_iota(jnp.int32, (SG_CHUNK, 4 * SG_CHUNK), 0)
    col = lax.broadcasted_iota(jnp.int32, (SG_CHUNK, 4 * SG_CHUNK), 1) % SG_CHUNK
    sg_w = jnp.where(row >= col, sg_w_ref[...], 0.0).astype(BF16)
    for c in range(TS // SG_CHUNK):
        _sgu_chunk(c, z_a, sg_lng_ref[...], sg_w, sg_bias_ref[...], ya_ref)


def _front(layer, x3, g, w_all, sg_lng, sg_w, sg_bias, cw, cb, cv_lng, cv_lnb, pw, pwb, pool_w, pool_scale):
    b, s, _ = x3.shape
    const = lambda shape: pl.BlockSpec(shape, lambda bi, si: (0,) * len(shape))
    w_spec = pl.BlockSpec((None, D_MODEL, Z_COLS), lambda bi, si: (layer, 0, 0))
    tile = lambda width: pl.BlockSpec((1, TS, width), lambda bi, si: (bi, si, 0))
    pool_rows = POOL_PAD + POOL_HALO + TS
    return pl.pallas_call(
        _front_kernel,
        grid=(b, s // TS),
        in_specs=[
            tile(D_MODEL), const((1, D_MODEL)), w_spec,
            const((1, GW)), const((SG_CHUNK, 4 * SG_CHUNK)), const((SG_CHUNK, GW)),
            const((CONV_WIDTH, GW)), const((1, GW)), const((1, GW)), const((1, GW)),
            const((GW, GW)), const((1, GW)),
            const((GW, GW)), const((1, GW)),
        ],
        out_specs=[tile(GW), tile(GW), tile(GW), tile(GW), tile(KV_COLS)],
        out_shape=[
            jax.ShapeDtypeStruct((b, s, GW), BF16),
            jax.ShapeDtypeStruct((b, s, GW), BF16),
            jax.ShapeDtypeStruct((b, s, GW), BF16),
            jax.ShapeDtypeStruct((b, s, GW), F32),
            jax.ShapeDtypeStruct((b, s, KV_COLS), F32),
        ],
        scratch_shapes=[pltpu.VMEM((SUBLANES, CONV_HALO + TS, GW), F32)]
        + [pltpu.VMEM((pool_rows, GW), F32)] * 4,
        compiler_params=_cparams(("parallel", "arbitrary")),
        name="front",
    )(x3, g, w_all, sg_lng, sg_w, sg_bias, cw, cb, cv_lng, cv_lnb, pw, pwb, pool_w, pool_scale)


def _compress_kernel(x_ref, pe_ref, w1_ref, w2_ref, o_ref):
    m = CMP_TB * N_CMP_PAD
    p0 = jnp.zeros((m, 128), F32)
    p1 = jnp.zeros((m, 128), F32)
    for r in range(CMP_STRIDE):
        xr = jnp.concatenate([x_ref[bi, pl.ds(r, N_CMP_PAD, stride=CMP_STRIDE), :] for bi in range(CMP_TB)],
                             axis=0)
        p0 = p0 + _dot((xr + pe_ref[r:r + 1, :]).astype(BF16), w1_ref[r])
        p1 = p1 + _dot((xr + pe_ref[CMP_STRIDE + r:CMP_STRIDE + r + 1, :]).astype(BF16),
                       w1_ref[CMP_STRIDE + r])
    pre = p0 + pltpu.roll(p1, m - 1, axis=0)
    hid = pre * jax.nn.sigmoid(pre)
    out = _dot(hid.astype(BF16), w2_ref[...])
    rowid = lax.broadcasted_iota(jnp.int32, (m, 128), 0) % N_CMP_PAD
    out = jnp.where(rowid < N_CMP_PAD - 1, out, 0.0)
    o_ref[...] = out.reshape(CMP_TB, N_CMP_PAD, 128)


def _compress(zkv, pe, w1, w2):
    b, s, _ = zkv.shape
    return pl.pallas_call(
        _compress_kernel,
        grid=(b // CMP_TB,),
        in_specs=[
            pl.BlockSpec((CMP_TB, s, 128), lambda i: (i, 0, 0)),
            pl.BlockSpec((CMP_BLOCK, 128), lambda i: (0, 0)),
            pl.BlockSpec((CMP_BLOCK, 128, 128), lambda i: (0, 0, 0)),
            pl.BlockSpec((128, 128), lambda i: (0, 0)),
        ],
        out_specs=pl.BlockSpec((CMP_TB, N_CMP_PAD, 128), lambda i: (i, 0, 0)),
        out_shape=jax.ShapeDtypeStruct((b, N_CMP_PAD, 128), F32),
        compiler_params=_cparams(("parallel",)),
        name="nsa_compress",
    )(zkv, pe, w1, w2)


def _nsa_kernel(q_ref, kv_ref, kcv_ref, o_ref, ks_aug, kts, vs_aug, kw_aug, vw_aug, gsel):
    i = pl.program_id(1)
    seq = kv_ref.shape[1]
    nq = Q_BLOCK

    @pl.when(i == 0)
    def _():
        eye = jnp.where(lax.broadcasted_iota(jnp.int32, (AUG, AUG), 0)
                        == lax.broadcasted_iota(jnp.int32, (AUG, AUG), 1), 1.0, 0.0).astype(BF16)
        lane_p = lax.broadcasted_iota(jnp.int32, (WIN, AUG), 1)
        kw_aug[0:WIN, :] = jnp.where(lane_p == PAD_LANE, 1.0, 0.0).astype(BF16)
        vw_aug[0:WIN, :] = jnp.zeros((WIN, 2 * AUG), BF16)
        gsel[...] = jnp.where(lax.broadcasted_iota(jnp.int32, gsel.shape, 0)
                              == lax.broadcasted_iota(jnp.int32, gsel.shape, 1) // AUG, 1.0, 0.0).astype(BF16)
        lane = lax.broadcasted_iota(jnp.int32, (256, AUG), 1)
        ones = jnp.ones((256, AUG), BF16)
        for r in range(0, seq, 256):
            blk = kv_ref[0, r:r + 256, :]
            t_s = blk[:, 128:256]
            t_w = blk[:, 256:384]
            key_blk = (r + lax.broadcasted_iota(jnp.int32, (256, AUG), 0)) // SLC_BLOCK
            onehot = jnp.where(lane - HEAD_DIM == key_blk, 1.0, 0.0)
            ks_aug[r:r + 256, :] = jnp.where(lane < HEAD_DIM, t_s, onehot).astype(BF16)
            v_s = jnp.where(lane < HEAD_DIM, pltpu.roll(t_s, HEAD_DIM, axis=1), 0.0).astype(BF16)
            vs_aug[r:r + 256, :] = jnp.concatenate([v_s, ones], axis=1)
            kw_aug[WIN + r:WIN + r + 256, :] = jnp.where(lane < HEAD_DIM, t_w, 0.0).astype(BF16)
            v_w = jnp.where(lane < HEAD_DIM, pltpu.roll(t_w, HEAD_DIM, axis=1), 0.0).astype(BF16)
            vw_aug[WIN + r:WIN + r + 256, :] = jnp.concatenate([v_w, ones], axis=1)
        for c in range(0, seq, SLC_TK):
            kts[:, c:c + SLC_TK] = _nt(eye, ks_aug[c:c + SLC_TK, :]).astype(BF16)

    per = SLC_TK // nq
    n_dyn = (i + per - 1) // per
    for n_kt in range(seq // SLC_TK + 1):
        @pl.when(n_dyn == n_kt)
        def _(n_kt=n_kt):
            _nsa_tile(n_kt, i, q_ref, kv_ref, kcv_ref, o_ref, ks_aug, kts, vs_aug, kw_aug, vw_aug, gsel)


def _nsa_tile(n_kt, i, q_ref, kv_ref, kcv_ref, o_ref, ks_aug, kts, vs_aug, kw_aug, vw_aug, gsel):
    nq = Q_BLOCK
    hq = N_HEADS * nq
    eye = jnp.where(lax.broadcasted_iota(jnp.int32, (AUG, AUG), 0)
                    == lax.broadcasted_iota(jnp.int32, (AUG, AUG), 1), 1.0, 0.0).astype(BF16)
    q0 = pl.multiple_of(i * nq, nq)
    q = q_ref[0] * (HEAD_DIM ** -0.5 * LOG2E)
    lane_q = lax.broadcasted_iota(jnp.int32, (nq, AUG), 1)

    def stack_heads(extra):
        tiles = []
        for h in range(N_HEADS):
            t = q[:, AUG * (h // 2):AUG * (h // 2 + 1)]
            if h % 2:
                t = pltpu.roll(t, HEAD_DIM, axis=1)
            tiles.append(jnp.where(lane_q < HEAD_DIM, t, extra))
        return jnp.concatenate(tiles, axis=0).astype(BF16)

    pad_bias = jnp.where(lane_q == PAD_LANE, NEG, 0.0)
    qw = stack_heads(pad_bias)
    trow = q0 + lax.broadcasted_iota(jnp.int32, (nq, 1), 0)
    row_l = lax.broadcasted_iota(jnp.int32, (nq, nq), 0)
    col_l = lax.broadcasted_iota(jnp.int32, (nq, nq), 1)
    tri_le = (col_l <= row_l)[None]
    tri_gt = (col_l > row_l)[None]

    nk = WIN + nq
    sw = _nt(qw, kw_aug[pl.ds(q0, nk), :]).reshape(N_HEADS, nq, nk)
    sw = jnp.concatenate([jnp.where(tri_gt, sw[:, :, :nq], NEG), sw[:, :, nq:WIN],
                          jnp.where(tri_le, sw[:, :, WIN:], NEG)], axis=-1)
    pw = jnp.exp2(sw - jnp.max(sw, axis=-1, keepdims=True)).astype(BF16)
    o_win = _dot(pw.reshape(hq, nk), vw_aug[pl.ds(q0, nk), :])

    kcv = kcv_ref[0]
    lane_c = lax.broadcasted_iota(jnp.int32, (N_CMP_PAD, AUG), 1)
    kc = jnp.where(lane_c < HEAD_DIM, kcv, 0.0).astype(BF16)
    vc = jnp.where(lane_c < HEAD_DIM, pltpu.roll(kcv, HEAD_DIM, axis=1), 0.0).astype(BF16)
    sd = _nt(qw, ks_aug[pl.ds(q0, nq), :]).reshape(N_HEADS, nq, nq)
    sd = jnp.where(tri_le, sd, NEG).reshape(hq, nq)
    md = jnp.max(sd, axis=-1, keepdims=True)

    s = _nt(qw, kc).reshape(N_HEADS, nq, N_CMP_PAD)
    cidx = lax.broadcasted_iota(jnp.int32, (nq, N_CMP_PAD), 1)
    cmask = (cidx * CMP_STRIDE + (CMP_BLOCK - 1) <= trow) & (cidx < N_CMP_PAD - 1)
    sm = jnp.where(cmask[None], s, NEG)
    e = jnp.exp2(sm - jnp.max(sm, axis=-1, keepdims=True))
    p = e / jnp.sum(e, axis=-1, keepdims=True)
    p = jnp.where((trow >= CMP_BLOCK - 1)[None], p, 0.0)
    o_cmp = _dot(p.reshape(hq, N_CMP_PAD).astype(BF16), vc)

    psum = jnp.sum(p, axis=0)
    p_hi = psum.astype(BF16)
    p_lo = (psum - p_hi.astype(F32)).astype(BF16)
    jrow = lax.broadcasted_iota(jnp.int32, (N_SLC, N_CMP_PAD), 0)
    ccol = lax.broadcasted_iota(jnp.int32, (N_SLC, N_CMP_PAD), 1)
    ov = ((ccol * CMP_STRIDE <= jrow * SLC_BLOCK + (SLC_BLOCK - 1))
          & (ccol * CMP_STRIDE + (CMP_BLOCK - 1) >= jrow * SLC_BLOCK)
          & (ccol < N_CMP_PAD - 1))
    ov = jnp.where(ov, 1.0, 0.0).astype(BF16)
    imp = _nt(ov, p_hi) + _nt(ov, p_lo)
    jj = lax.broadcasted_iota(jnp.int32, (N_SLC, nq), 0)
    tt = q0 + lax.broadcasted_iota(jnp.int32, (N_SLC, nq), 1)
    cur = tt // SLC_BLOCK
    valid = jj * SLC_BLOCK <= tt
    forced = (jj == 0) | (jj == cur) | (jj == cur - 1)
    score = jnp.where(valid, imp + jnp.where(forced, FORCE_BONUS, 0.0), NEG)
    rank = jnp.zeros((N_SLC, nq), F32)
    for j2 in range(N_SLC):
        other = score[j2:j2 + 1, :]
        beats = (other > score) | ((other == score) & (jj > j2))
        rank = rank + jnp.where(beats, 1.0, 0.0)
    sel_t = jnp.where((rank < SLC_TOPK) & (score > NEG / 2), 1.0, 0.0)
    bias_t = jnp.where((sel_t > 0.5) & (jj < 2 * i), 0.0, NEG)
    bias_t = jnp.concatenate([jnp.zeros((HEAD_DIM, nq), F32), bias_t,
                              jnp.zeros((AUG - HEAD_DIM - N_SLC, nq), F32)], axis=0).astype(BF16)
    blk_bias = _nt(eye, bias_t)

    if n_kt:
        n_main = n_kt * SLC_TK
        qs = stack_heads(blk_bias + pad_bias)
        s_main = _dot(qs, kts[:, 0:n_main])
        m = jnp.maximum(md, jnp.max(s_main, axis=-1, keepdims=True))
        p_main = jnp.exp2(s_main - m).astype(BF16)
        p_diag = jnp.exp2(sd - m).astype(BF16)
        o_slc = _dot(p_diag, vs_aug[pl.ds(q0, nq), :]) + _dot(p_main, vs_aug[0:n_main, :])
    else:
        o_slc = _dot(jnp.exp2(sd - md).astype(BF16), vs_aug[pl.ds(q0, nq), :])

    g = jax.nn.sigmoid(kv_ref[0, pl.ds(q0, nq), 384:384 + AUG])
    g_hi = g.astype(BF16)
    g_lo = (g - g_hi.astype(F32)).astype(BF16)
    g_rep = _dot(g_hi, gsel[...]) + _dot(g_lo, gsel[...])
    gate = lambda br: jnp.concatenate(
        [g_rep[:, (3 * h + br) * AUG:(3 * h + br + 1) * AUG] for h in range(N_HEADS)], axis=0)
    o = (gate(0) * o_cmp
         + (gate(1) / o_slc[:, AUG:]) * o_slc[:, :AUG]
         + (gate(2) / o_win[:, AUG:]) * o_win[:, :AUG])
    halves = [jnp.where(lane_q < HEAD_DIM, o[2 * a * nq:(2 * a + 1) * nq, :],
                        pltpu.roll(o[(2 * a + 1) * nq:(2 * a + 2) * nq, :], HEAD_DIM, axis=1))
              for a in range(N_HEADS // 2)]
    o_ref[0] = jnp.concatenate(halves, axis=1).astype(BF16)


def _nsa(zq, zkv, kcv):
    b, s, _ = zq.shape
    return pl.pallas_call(
        _nsa_kernel,
        grid=(b, s // Q_BLOCK),
        in_specs=[
            pl.BlockSpec((1, Q_BLOCK, GW), lambda bi, qi: (bi, qi, 0)),
            pl.BlockSpec((1, s, KV_COLS), lambda bi, qi: (bi, 0, 0)),
            pl.BlockSpec((1, N_CMP_PAD, 128), lambda bi, qi: (bi, 0, 0)),
        ],
        out_specs=pl.BlockSpec((1, Q_BLOCK, GW), lambda bi, qi: (bi, qi, 0)),
        out_shape=jax.ShapeDtypeStruct((b, s, GW), BF16),
        scratch_shapes=[
            pltpu.VMEM((s, AUG), BF16),
            pltpu.VMEM((AUG, s), BF16),
            pltpu.VMEM((s, 2 * AUG), BF16),
            pltpu.VMEM((WIN + s, AUG), BF16),
            pltpu.VMEM((WIN + s, 2 * AUG), BF16),
            pltpu.VMEM((AUG, 3 * N_HEADS * AUG), BF16),
        ],
        compiler_params=_cparams(("parallel", "arbitrary")),
        name="nsa_attn",
    )(zq, zkv, kcv)


def _back_kernel(ya_ref, yb_ref, yc_ref, yd_ref, x_ref, wo_ref, gmix_ref, gpre_ref, wgu_ref, wd_ref,
                 gpost_ref, o_ref):
    y = jnp.concatenate([ya_ref[...], yb_ref[...], yc_ref[...], yd_ref[...]], axis=1)
    x = x_ref[...] + _rms(_dot(y, wo_ref[...]), gmix_ref[...])
    h = _rms(x, gpre_ref[...]).astype(BF16)
    f = jnp.zeros((TM, D_MODEL), F32)
    c0 = 0
    for width in FFN_CHUNKS:
        gate = _dot(h, wgu_ref[:, c0:c0 + width])
        up = _dot(h, wgu_ref[:, FFN_HIDDEN + c0:FFN_HIDDEN + c0 + width])
        act = ((gate * jax.nn.sigmoid(gate)) * up).astype(BF16)
        f = f + _dot(act, wd_ref[c0:c0 + width, :])
        c0 += width
    o_ref[...] = x + _rms(f, gpost_ref[...])


def _back(layer, ya, yb, yc, yd, x2, wo_all, gmix, gpre, wgu_all, wd_all, gpost):
    n = x2.shape[0]
    const = lambda shape: pl.BlockSpec(shape, lambda i: (0, 0), pipeline_mode=pl.Buffered(1))
    slab = lambda rows, cols: pl.BlockSpec((None, rows, cols), lambda i: (layer, 0, 0),
                                           pipeline_mode=pl.Buffered(1))
    yspec = pl.BlockSpec((TM, GW), lambda i: (i, 0))
    return pl.pallas_call(
        _back_kernel,
        grid=(n // TM,),
        in_specs=[
            yspec, yspec, yspec, yspec,
            pl.BlockSpec((TM, D_MODEL), lambda i: (i, 0)),
            slab(D_MODEL, D_MODEL), const((1, D_MODEL)), const((1, D_MODEL)),
            slab(D_MODEL, 2 * FFN_HIDDEN), slab(FFN_HIDDEN, D_MODEL),
            const((1, D_MODEL)),
        ],
        out_specs=pl.BlockSpec((TM, D_MODEL), lambda i: (i, 0)),
        out_shape=jax.ShapeDtypeStruct((n, D_MODEL), F32),
        compiler_params=_cparams(("parallel",)),
        name="back",
    )(ya, yb, yc, yd, x2, wo_all, gmix, gpre, wgu_all, wd_all, gpost)


def _interleave(wk, wv):
    z = jnp.zeros((CMP_BLOCK, HEAD_DIM, HEAD_DIM), wk.dtype)
    wk3 = wk.reshape(CMP_BLOCK, HEAD_DIM, HEAD_DIM)
    wv3 = wv.reshape(CMP_BLOCK, HEAD_DIM, HEAD_DIM)
    return jnp.concatenate([jnp.concatenate([wk3, z], axis=2), jnp.concatenate([z, wv3], axis=2)], axis=1)


def _block_diag(mats):
    n = len(mats)
    rows = []
    for a, m in enumerate(mats):
        rows.append(jnp.concatenate([m if a == c else jnp.zeros_like(m) for c in range(n)], axis=1))
    return jnp.concatenate(rows, axis=0)


def _layer(layer, x2, batch, seq, p, big):
    n = x2.shape[0]
    row = lambda v: v.reshape(1, -1)
    sg_w = jnp.transpose(p["sg_w"], (1, 0, 2)).reshape(SG_CHUNK, N_HEADS * SG_CHUNK)
    sg_bias = jnp.repeat(p["sg_b"].T, HEAD_DIM, axis=1)
    pool_w = _block_diag([p["pool_w"][gi] for gi in range(len(POOL_WINDOWS))]).astype(BF16)
    y_a, y_b, y_d, zq, zkv = _front(
        layer, x2.reshape(batch, seq, D_MODEL), row(p["g_pre_mix"]), big["w_in"],
        row(p["sg_ln_g"]), sg_w, sg_bias,
        p["cv_w"], row(p["cv_b"]), row(p["cv_ln_g"]), row(p["cv_ln_b"]), p["cv_pw"].astype(BF16),
        row(p["cv_pw_b"]), pool_w, row(p["pool_scale"]))

    pe = jnp.concatenate([p["cmp_pos_k"], p["cmp_pos_v"]], axis=1)
    w1 = _interleave(p["cmp_w1_k"], p["cmp_w1_v"]).astype(BF16)
    w2 = _block_diag([p["cmp_w2_k"], p["cmp_w2_v"]]).astype(BF16)
    kcv = _compress(zkv, pe, w1, w2)
    y_c = _nsa(zq, zkv, kcv)

    flat = lambda y: y.reshape(n, GW)
    return _back(layer, flat(y_a), flat(y_b), flat(y_c), flat(y_d), x2,
                 big["w_out"], row(p["g_post_mix"]), row(p["g_pre_ffn"]),
                 big["ffn_w_gu"], big["ffn_w_down"], row(p["g_post_ffn"]))


_PARAM_NAMES = ("g_pre_mix", "g_post_mix", "g_pre_ffn", "g_post_ffn", "w_in", "sg_ln_g", "sg_w", "sg_b",
                "cv_w", "cv_b", "cv_ln_g", "cv_ln_b", "cv_pw", "cv_pw_b", "cmp_pos_k", "cmp_pos_v",
                "cmp_w1_k", "cmp_w2_k", "cmp_w1_v", "cmp_w2_v", "pool_w", "pool_scale", "w_out",
                "ffn_w_gu", "ffn_w_down")


def kernel(x, g_pre_mix, g_post_mix, g_pre_ffn, g_post_ffn, w_in, sg_ln_g, sg_w, sg_b, cv_w, cv_b, cv_ln_g, cv_ln_b, cv_pw, cv_pw_b, cmp_pos_k, cmp_pos_v, cmp_w1_k, cmp_w2_k, cmp_w1_v, cmp_w2_v, pool_w, pool_scale, w_out, ffn_w_gu, ffn_w_down):
    params = dict(zip(_PARAM_NAMES, (g_pre_mix, g_post_mix, g_pre_ffn, g_post_ffn, w_in, sg_ln_g, sg_w,
                                     sg_b, cv_w, cv_b, cv_ln_g, cv_ln_b, cv_pw, cv_pw_b, cmp_pos_k,
                                     cmp_pos_v, cmp_w1_k, cmp_w2_k, cmp_w1_v, cmp_w2_v, pool_w,
                                     pool_scale, w_out, ffn_w_gu, ffn_w_down)))
    batch, seq, _ = x.shape
    n_layers = g_pre_mix.shape[0]
    w_in_all = jnp.concatenate(
        [w_in[:, :, :1280], w_in[:, :, 1676:1932], w_in[:, :, 1280:1676],
         jnp.zeros((n_layers, D_MODEL, Z_COLS - 1932), w_in.dtype)], axis=2).astype(BF16)
    big = {"w_in": w_in_all, "w_out": w_out.astype(BF16), "ffn_w_gu": ffn_w_gu.astype(BF16),
           "ffn_w_down": ffn_w_down.astype(BF16)}
    small = {k: v for k, v in params.items() if k not in big}
    x2 = x.reshape(batch * seq, D_MODEL)
    for layer in range(n_layers):
        x2 = _layer(layer, x2, batch, seq, {k: v[layer] for k, v in small.items()}, big)
    return x2.reshape(batch, seq, D_MODEL)
```

```python
import jax
import jax.numpy as jnp
from jax import lax
from jax.experimental import pallas as pl
from jax.experimental.pallas import tpu as pltpu

F32 = jnp.float32
BF16 = jnp.bfloat16

D_MODEL = 1024
GW = 256
HEAD_DIM = 64
N_HEADS = 4
SG_CHUNK = 128
CONV_WIDTH = 31
CMP_BLOCK = 32
CMP_STRIDE = 16
SLC_BLOCK = 64
SLC_TOPK = 8
WIN = 512
Q_BLOCK = 128
FORCE_BONUS = 1e4
NEG = -1e30
POOL_WINDOWS = (2, 4, 8, 16)
FFN_HIDDEN = 2816
RMS_EPS = 1e-6
LN_EPS = 1e-5
Z_COLS = 2048
Q_OFF = 1024
D_OFF = 1280
KV_OFF = 1536
KV_COLS = 512
N_CMP_PAD = 128
N_SLC = 32

TM = 512
TS = 512
CONV_HALO = 32
POOL_HALO = 16
SUB = 64
SUBLANES = 8
POOL_PAD = SUBLANES
SHIFT_CHUNK = 128
LOG2E = 1.4426950408889634
SLC_TK = 512
SLC_ROW_GROUPS = 2
AUG = 128
PAD_LANE = HEAD_DIM + N_SLC
MXU_DIM = 256
FFN_CHUNKS = (6 * MXU_DIM, 5 * MXU_DIM)
CMP_TB = 4
VMEM_LIMIT = 56 * 1024 * 1024


def _cparams(sem):
    return pltpu.CompilerParams(dimension_semantics=sem, vmem_limit_bytes=VMEM_LIMIT)


def _nt(a, b):
    return lax.dot_general(a, b, (((1,), (1,)), ((), ())), preferred_element_type=F32)


def _dot(a, b):
    return jnp.dot(a, b, preferred_element_type=F32)


def _rms(x, g):
    return (x * lax.rsqrt(jnp.mean(x * x, axis=-1, keepdims=True) + RMS_EPS)) * g


def _layernorm(x, g):
    mu = jnp.mean(x, axis=-1, keepdims=True)
    d = x - mu
    var = jnp.mean(d * d, axis=-1, keepdims=True)
    return (d * lax.rsqrt(var + LN_EPS)) * g


def _sgu_chunk(c, z_a, lng, w, bias, o_ref):
    lane_head = lax.broadcasted_iota(jnp.int32, (SG_CHUNK, GW), 1) // HEAD_DIM
    blk = z_a[c * SG_CHUNK:(c + 1) * SG_CHUNK, :]
    u = blk[:, :GW]
    vn = _layernorm(blk[:, GW:], lng).astype(BF16)
    zero = jnp.zeros_like(vn)
    v4 = jnp.concatenate([jnp.where(lane_head == h, vn, zero) for h in range(N_HEADS)], axis=0)
    sv = _dot(w, v4) + bias
    o_ref[0, c * SG_CHUNK:(c + 1) * SG_CHUNK, :] = (u * sv).astype(BF16)


def _conv_prep(si, z_b, hbuf):
    prev = hbuf[0, TS:TS + CONV_HALO, :]
    hbuf[0, 0:CONV_HALO, :] = jnp.where(si > 0, prev, 0.0)
    for r0 in range(0, TS, SUB):
        blk = z_b[r0:r0 + SUB, :]
        hbuf[0, CONV_HALO + r0:CONV_HALO + r0 + SUB, :] = blk[:, :GW] * jax.nn.sigmoid(blk[:, GW:])
    rows = CONV_HALO + TS - SUBLANES
    for r in range(1, SUBLANES):
        for j0 in range(0, rows, SHIFT_CHUNK):
            n = min(SHIFT_CHUNK, rows - j0)
            hbuf[r, j0:j0 + n, :] = hbuf[0, j0 + r:j0 + r + n, :]


def _conv_chunk(r0, cw, cb, lng, lnb, pw, pwb, o_ref, hbuf):
    lead = CONV_HALO - (CONV_WIDTH - 1)
    acc = jnp.zeros((SUB, GW), F32)
    for k in range(CONV_WIDTH):
        off = lead + k
        base = r0 + off - off % SUBLANES
        acc = acc + cw[k:k + 1, :] * hbuf[off % SUBLANES, base:base + SUB, :]
    y = _layernorm(acc + cb, lng) + lnb
    y = y * jax.nn.sigmoid(y)
    out = _dot(y.astype(BF16), pw) + pwb
    o_ref[0, r0:r0 + SUB, :] = out.astype(BF16)


def _pool_prep(si, z_d, xbuf, s2buf, s4buf, s8buf):
    data0 = POOL_PAD + POOL_HALO
    total = data0 + TS
    zeros = jnp.zeros((POOL_PAD, GW), F32)
    prev = xbuf[total - POOL_HALO:total, :]
    xbuf[0:POOL_PAD, :] = zeros
    s2buf[0:POOL_PAD, :] = zeros
    s4buf[0:POOL_PAD, :] = zeros
    xbuf[POOL_PAD:data0, :] = jnp.where(si > 0, prev, 0.0)
    xbuf[data0:total, :] = z_d
    for src, dst, shift in ((xbuf, s2buf, 1), (s2buf, s4buf, 2), (s4buf, s8buf, 4)):
        for j0 in range(POOL_PAD, total, SHIFT_CHUNK):
            n = min(SHIFT_CHUNK, total - j0)
            dst[j0:j0 + n, :] = src[j0:j0 + n, :] + src[j0 - shift:j0 - shift + n, :]


def _pool_chunk(si, r0, w, scale, o_ref, xbuf, s2buf, s4buf, s8buf):
    lane_grp = lax.broadcasted_iota(jnp.int32, (1, GW), 1) // (GW // len(POOL_WINDOWS))
    win = jnp.zeros((1, GW), jnp.int32)
    for gi, wlen in enumerate(POOL_WINDOWS):
        win = jnp.where(lane_grp == gi, wlen, win)
    j = POOL_PAD + POOL_HALO + r0
    x = xbuf[j:j + SUB, :]
    s8 = s8buf[j:j + SUB, :]
    s16 = s8 + s8buf[j - 8:j - 8 + SUB, :]
    acc = jnp.where(lane_grp == 0, s2buf[j:j + SUB, :],
                    jnp.where(lane_grp == 1, s4buf[j:j + SUB, :],
                              jnp.where(lane_grp == 2, s8, s16)))
    t1 = si * TS + r0 + 1 + lax.broadcasted_iota(jnp.int32, (SUB, GW), 0)
    cnt = jnp.minimum(t1, win).astype(F32)
    diff = acc / cnt - x
    out = _dot(diff.astype(BF16), w) * scale
    o_ref[0, r0:r0 + SUB, :] = out.astype(BF16)


def _front_kernel(x_ref, g_ref, w_ref,
                  sg_lng_ref, sg_w_ref, sg_bias_ref,
                  cw_ref, cb_ref, cv_lng_ref, cv_lnb_ref, pw_ref, pwb_ref,
                  pool_w_ref, pool_scale_ref,
                  ya_ref, yb_ref, yd_ref, zq_ref, zkv_ref,
                  hbuf, xbuf, s2buf, s4buf, s8buf):
    si = pl.program_id(1)

    @pl.when(si == 0)
    def _():
        hbuf[0, TS:TS + CONV_HALO, :] = jnp.zeros((CONV_HALO, GW), F32)
        xbuf[POOL_PAD + TS:POOL_PAD + POOL_HALO + TS, :] = jnp.zeros((POOL_HALO, GW), F32)

    conv_args = (cw_ref[...], cb_ref[...], cv_lng_ref[...], cv_lnb_ref[...], pw_ref[...], pwb_ref[...],
                 yb_ref, hbuf)
    pool_bufs = (xbuf, s2buf, s4buf, s8buf)
    conv = lambda k: _conv_chunk(k * SUB, *conv_args)
    pool = lambda k: _pool_chunk(si, k * SUB, pool_w_ref[...], pool_scale_ref[...], yd_ref, *pool_bufs)

    h = _rms(x_ref[0], g_ref[...]).astype(BF16)
    _conv_prep(si, _dot(h, w_ref[:, 2 * GW:4 * GW]), hbuf)
    z_qd = _dot(h, w_ref[:, Q_OFF:Q_OFF + 2 * GW])
    conv(0)
    conv(1)
    zq_ref[0] = z_qd[:, :GW]
    _pool_prep(si, z_qd[:, GW:], *pool_bufs)
    z_a = _dot(h, w_ref[:, 0:2 * GW])
    conv(2)
    conv(3)
    pool(0)
    pool(1)
    pool(2)
    pool(3)
    zkv_ref[0] = _dot(h, w_ref[:, KV_OFF:KV_OFF + KV_COLS])
    conv(4)
    conv(5)
    pool(4)
    pool(5)
    conv(6)
    conv(7)
    pool(6)
    pool(7)
    row = lax.broadcasted_iota(jnp.int32, (SG_CHUNK, 4 * SG_CHUNK), 0)
    col = lax.broadcasted_iota(jnp.int32, (SG_CHUNK, 4 * SG_CHUNK), 1) % SG_CHUNK
    sg_w = jnp.where(row >= col, sg_w_ref[...], 0.0).astype(BF16)
    for c in range(TS // SG_CHUNK):
        _sgu_chunk(c, z_a, sg_lng_ref[...], sg_w, sg_bias_ref[...], ya_ref)


def _front(layer, x3, g, w_all, sg_lng, sg_w, sg_bias, cw, cb, cv_lng, cv_lnb, pw, pwb, pool_w, pool_scale):
    b, s, _ = x3.shape
    const = lambda shape: pl.BlockSpec(shape, lambda bi, si: (0,) * len(shape))
    w_spec = pl.BlockSpec((None, D_MODEL, Z_COLS), lambda bi, si: (layer, 0, 0))
    tile = lambda width: pl.BlockSpec((1, TS, width), lambda bi, si: (bi, si, 0))
    pool_rows = POOL_PAD + POOL_HALO + TS
    return pl.pallas_call(
        _front_kernel,
        grid=(b, s // TS),
        in_specs=[
            tile(D_MODEL), const((1, D_MODEL)), w_spec,
            const((1, GW)), const((SG_CHUNK, 4 * SG_CHUNK)), const((SG_CHUNK, GW)),
            const((CONV_WIDTH, GW)), const((1, GW)), const((1, GW)), const((1, GW)),
            const((GW, GW)), const((1, GW)),
            const((GW, GW)), const((1, GW)),
        ],
        out_specs=[tile(GW), tile(GW), tile(GW), tile(GW), tile(KV_COLS)],
        out_shape=[
            jax.ShapeDtypeStruct((b, s, GW), BF16),
            jax.ShapeDtypeStruct((b, s, GW), BF16),
            jax.ShapeDtypeStruct((b, s, GW), BF16),
            jax.ShapeDtypeStruct((b, s, GW), F32),
            jax.ShapeDtypeStruct((b, s, KV_COLS), F32),
        ],
        scratch_shapes=[pltpu.VMEM((SUBLANES, CONV_HALO + TS, GW), F32)]
        + [pltpu.VMEM((pool_rows, GW), F32)] * 4,
        compiler_params=_cparams(("parallel", "arbitrary")),
        name="front",
    )(x3, g, w_all, sg_lng, sg_w, sg_bias, cw, cb, cv_lng, cv_lnb, pw, pwb, pool_w, pool_scale)


def _compress_kernel(x_ref, pe_ref, w1_ref, w2_ref, o_ref):
    m = CMP_TB * N_CMP_PAD
    p0 = jnp.zeros((m, 128), F32)
    p1 = jnp.zeros((m, 128), F32)
    for r in range(CMP_STRIDE):
        xr = jnp.concatenate([x_ref[bi, pl.ds(r, N_CMP_PAD, stride=CMP_STRIDE), :] for bi in range(CMP_TB)],
                             axis=0)
        p0 = p0 + _dot((xr + pe_ref[r:r + 1, :]).astype(BF16), w1_ref[r])
        p1 = p1 + _dot((xr + pe_ref[CMP_STRIDE + r:CMP_STRIDE + r + 1, :]).astype(BF16),
                       w1_ref[CMP_STRIDE + r])
    pre = p0 + pltpu.roll(p1, m - 1, axis=0)
    hid = pre * jax.nn.sigmoid(pre)
    out = _dot(hid.astype(BF16), w2_ref[...])
    rowid = lax.broadcasted_iota(jnp.int32, (m, 128), 0) % N_CMP_PAD
    out = jnp.where(rowid < N_CMP_PAD - 1, out, 0.0)
    o_ref[...] = out.reshape(CMP_TB, N_CMP_PAD, 128)


def _compress(zkv, pe, w1, w2):
    b, s, _ = zkv.shape
    return pl.pallas_call(
        _compress_kernel,
        grid=(b // CMP_TB,),
        in_specs=[
            pl.BlockSpec((CMP_TB, s, 128), lambda i: (i, 0, 0)),
            pl.BlockSpec((CMP_BLOCK, 128), lambda i: (0, 0)),
            pl.BlockSpec((CMP_BLOCK, 128, 128), lambda i: (0, 0, 0)),
            pl.BlockSpec((128, 128), lambda i: (0, 0)),
        ],
        out_specs=pl.BlockSpec((CMP_TB, N_CMP_PAD, 128), lambda i: (i, 0, 0)),
        out_shape=jax.ShapeDtypeStruct((b, N_CMP_PAD, 128), F32),
        compiler_params=_cparams(("parallel",)),
        name="nsa_compress",
    )(zkv, pe, w1, w2)


def _nsa_kernel(q_ref, kv_ref, kcv_ref, o_ref, ks_aug, kts, vs_aug, kw_aug, vw_aug, gsel):
    i = pl.program_id(1)
    seq = kv_ref.shape[1]
    nq = Q_BLOCK

    @pl.when(i == 0)
    def _():
        eye = jnp.where(lax.broadcasted_iota(jnp.int32, (AUG, AUG), 0)
                        == lax.broadcasted_iota(jnp.int32, (AUG, AUG), 1), 1.0, 0.0).astype(BF16)
        lane_p = lax.broadcasted_iota(jnp.int32, (WIN, AUG), 1)
        kw_aug[0:WIN, :] = jnp.where(lane_p == PAD_LANE, 1.0, 0.0).astype(BF16)
        vw_aug[0:WIN, :] = jnp.zeros((WIN, 2 * AUG), BF16)
        gsel[...] = jnp.where(lax.broadcasted_iota(jnp.int32, gsel.shape, 0)
                              == lax.broadcasted_iota(jnp.int32, gsel.shape, 1) // AUG, 1.0, 0.0).astype(BF16)
        lane = lax.broadcasted_iota(jnp.int32, (256, AUG), 1)
        ones = jnp.ones((256, AUG), BF16)
        for r in range(0, seq, 256):
            blk = kv_ref[0, r:r + 256, :]
            t_s = blk[:, 128:256]
            t_w = blk[:, 256:384]
            key_blk = (r + lax.broadcasted_iota(jnp.int32, (256, AUG), 0)) // SLC_BLOCK
            onehot = jnp.where(lane - HEAD_DIM == key_blk, 1.0, 0.0)
            ks_aug[r:r + 256, :] = jnp.where(lane < HEAD_DIM, t_s, onehot).astype(BF16)
            v_s = jnp.where(lane < HEAD_DIM, pltpu.roll(t_s, HEAD_DIM, axis=1), 0.0).astype(BF16)
            vs_aug[r:r + 256, :] = jnp.concatenate([v_s, ones], axis=1)
            kw_aug[WIN + r:WIN + r + 256, :] = jnp.where(lane < HEAD_DIM, t_w, 0.0).astype(BF16)
            v_w = jnp.where(lane < HEAD_DIM, pltpu.roll(t_w, HEAD_DIM, axis=1), 0.0).astype(BF16)
            vw_aug[WIN + r:WIN + r + 256, :] = jnp.concatenate([v_w, ones], axis=1)
        for c in range(0, seq, SLC_TK):
            kts[:, c:c + SLC_TK] = _nt(eye, ks_aug[c:c + SLC_TK, :]).astype(BF16)

    per = SLC_TK // nq
    n_dyn = (i + per - 1) // per
    for n_kt in range(seq // SLC_TK + 1):
        @pl.when(n_dyn == n_kt)
        def _(n_kt=n_kt):
            _nsa_tile(n_kt, i, q_ref, kv_ref, kcv_ref, o_ref, ks_aug, kts, vs_aug, kw_aug, vw_aug, gsel)


def _nsa_tile(n_kt, i, q_ref, kv_ref, kcv_ref, o_ref, ks_aug, kts, vs_aug, kw_aug, vw_aug, gsel):
    nq = Q_BLOCK
    hq = N_HEADS * nq
    eye = jnp.where(lax.broadcasted_iota(jnp.int32, (AUG, AUG), 0)
                    == lax.broadcasted_iota(jnp.int32, (AUG, AUG), 1), 1.0, 0.0).astype(BF16)
    q0 = pl.multiple_of(i * nq, nq)
    q = q_ref[0] * (HEAD_DIM ** -0.5 * LOG2E)
    lane_q = lax.broadcasted_iota(jnp.int32, (nq, AUG), 1)

    def stack_heads(extra):
        tiles = []
        for h in range(N_HEADS):
            t = q[:, AUG * (h // 2):AUG * (h // 2 + 1)]
            if h % 2:
                t = pltpu.roll(t, HEAD_DIM, axis=1)
            tiles.append(jnp.where(lane_q < HEAD_DIM, t, extra))
        return jnp.concatenate(tiles, axis=0).astype(BF16)

    pad_bias = jnp.where(lane_q == PAD_LANE, NEG, 0.0)
    qw = stack_heads(pad_bias)
    trow = q0 + lax.broadcasted_iota(jnp.int32, (nq, 1), 0)
    row_l = lax.broadcasted_iota(jnp.int32, (nq, nq), 0)
    col_l = lax.broadcasted_iota(jnp.int32, (nq, nq), 1)
    tri_le = (col_l <= row_l)[None]
    tri_gt = (col_l > row_l)[None]

    nk = WIN + nq
    sw = _nt(qw, kw_aug[pl.ds(q0, nk), :]).reshape(N_HEADS, nq, nk)
    sw = jnp.concatenate([jnp.where(tri_gt, sw[:, :, :nq], NEG), sw[:, :, nq:WIN],
                          jnp.where(tri_le, sw[:, :, WIN:], NEG)], axis=-1)
    pw = jnp.exp2(sw - jnp.max(sw, axis=-1, keepdims=True)).astype(BF16)
    o_win = _dot(pw.reshape(hq, nk), vw_aug[pl.ds(q0, nk), :])

    kcv = kcv_ref[0]
    lane_c = lax.broadcasted_iota(jnp.int32, (N_CMP_PAD, AUG), 1)
    kc = jnp.where(lane_c < HEAD_DIM, kcv, 0.0).astype(BF16)
    vc = jnp.where(lane_c < HEAD_DIM, pltpu.roll(kcv, HEAD_DIM, axis=1), 0.0).astype(BF16)
    sd = _nt(qw, ks_aug[pl.ds(q0, nq), :]).reshape(N_HEADS, nq, nq)
    sd = jnp.where(tri_le, sd, NEG).reshape(hq, nq)
    md = jnp.max(sd, axis=-1, keepdims=True)

    s = _nt(qw, kc).reshape(N_HEADS, nq, N_CMP_PAD)
    cidx = lax.broadcasted_iota(jnp.int32, (nq, N_CMP_PAD), 1)
    cmask = (cidx * CMP_STRIDE + (CMP_BLOCK - 1) <= trow) & (cidx < N_CMP_PAD - 1)
    sm = jnp.where(cmask[None], s, NEG)
    e = jnp.exp2(sm - jnp.max(sm, axis=-1, keepdims=True))
    p = e / jnp.sum(e, axis=-1, keepdims=True)
    p = jnp.where((trow >= CMP_BLOCK - 1)[None], p, 0.0)
    o_cmp = _dot(p.reshape(hq, N_CMP_PAD).astype(BF16), vc)

    psum = jnp.sum(p, axis=0)
    p_hi = psum.astype(BF16)
    p_lo = (psum - p_hi.astype(F32)).astype(BF16)
    jrow = lax.broadcasted_iota(jnp.int32, (N_SLC, N_CMP_PAD), 0)
    ccol = lax.broadcasted_iota(jnp.int32, (N_SLC, N_CMP_PAD), 1)
    ov = ((ccol * CMP_STRIDE <= jrow * SLC_BLOCK + (SLC_BLOCK - 1))
          & (ccol * CMP_STRIDE + (CMP_BLOCK - 1) >= jrow * SLC_BLOCK)
          & (ccol < N_CMP_PAD - 1))
    ov = jnp.where(ov, 1.0, 0.0).astype(BF16)
    imp = _nt(ov, p_hi) + _nt(ov, p_lo)
    jj = lax.broadcasted_iota(jnp.int32, (N_SLC, nq), 0)
    tt = q0 + lax.broadcasted_iota(jnp.int32, (N_SLC, nq), 1)
    cur = tt // SLC_BLOCK
    valid = jj * SLC_BLOCK <= tt
    forced = (jj == 0) | (jj == cur) | (jj == cur - 1)
    score = jnp.where(valid, imp + jnp.where(forced, FORCE_BONUS, 0.0), NEG)
    rank = jnp.zeros((N_SLC, nq), F32)
    for j2 in range(N_SLC):
        other = score[j2:j2 + 1, :]
        beats = (other > score) | ((other == score) & (jj > j2))
        rank = rank + jnp.where(beats, 1.0, 0.0)
    sel_t = jnp.where((rank < SLC_TOPK) & (score > NEG / 2), 1.0, 0.0)
    bias_t = jnp.where((sel_t > 0.5) & (jj < 2 * i), 0.0, NEG)
    bias_t = jnp.concatenate([jnp.zeros((HEAD_DIM, nq), F32), bias_t,
                              jnp.zeros((AUG - HEAD_DIM - N_SLC, nq), F32)], axis=0).astype(BF16)
    blk_bias = _nt(eye, bias_t)

    if n_kt:
        n_main = n_kt * SLC_TK
        qs = stack_heads(blk_bias + pad_bias)
        k_main = kts[:, 0:n_main]
        v_main = vs_aug[0:n_main, :]
        v_diag = vs_aug[pl.ds(q0, nq), :]
        n_grp = SLC_ROW_GROUPS if n_kt >= 2 else 1
        rows = hq // n_grp
        by_rows = lambda a, b: jnp.concatenate([_dot(a[r0:r0 + rows], b) for r0 in range(0, hq, rows)], axis=0)
        s_main = by_rows(qs, k_main)
        m = jnp.maximum(md, jnp.max(s_main, axis=-1, keepdims=True))
        p_main = jnp.exp2(s_main - m).astype(BF16)
        p_diag = jnp.exp2(sd - m).astype(BF16)
        o_slc = _dot(p_diag, v_diag) + by_rows(p_main, v_main)
    else:
        o_slc = _dot(jnp.exp2(sd - md).astype(BF16), vs_aug[pl.ds(q0, nq), :])

    g = jax.nn.sigmoid(kv_ref[0, pl.ds(q0, nq), 384:384 + AUG])
    g_hi = g.astype(BF16)
    g_lo = (g - g_hi.astype(F32)).astype(BF16)
    g_rep = _dot(g_hi, gsel[...]) + _dot(g_lo, gsel[...])
    gate = lambda br: jnp.concatenate(
        [g_rep[:, (3 * h + br) * AUG:(3 * h + br + 1) * AUG] for h in range(N_HEADS)], axis=0)
    o = (gate(0) * o_cmp
         + (gate(1) / o_slc[:, AUG:]) * o_slc[:, :AUG]
         + (gate(2) / o_win[:, AUG:]) * o_win[:, :AUG])
    halves = [jnp.where(lane_q < HEAD_DIM, o[2 * a * nq:(2 * a + 1) * nq, :],
                        pltpu.roll(o[(2 * a + 1) * nq:(2 * a + 2) * nq, :], HEAD_DIM, axis=1))
              for a in range(N_HEADS // 2)]
    o_ref[0] = jnp.concatenate(halves, axis=1).astype(BF16)


def _nsa(zq, zkv, kcv):
    b, s, _ = zq.shape
    return pl.pallas_call(
        _nsa_kernel,
        grid=(b, s // Q_BLOCK),
        in_specs=[
            pl.BlockSpec((1, Q_BLOCK, GW), lambda bi, qi: (bi, qi, 0)),
            pl.BlockSpec((1, s, KV_COLS), lambda bi, qi: (bi, 0, 0)),
            pl.BlockSpec((1, N_CMP_PAD, 128), lambda bi, qi: (bi, 0, 0)),
        ],
        out_specs=pl.BlockSpec((1, Q_BLOCK, GW), lambda bi, qi: (bi, qi, 0)),
        out_shape=jax.ShapeDtypeStruct((b, s, GW), BF16),
        scratch_shapes=[
            pltpu.VMEM((s, AUG), BF16),
            pltpu.VMEM((AUG, s), BF16),
            pltpu.VMEM((s, 2 * AUG), BF16),
            pltpu.VMEM((WIN + s, AUG), BF16),
            pltpu.VMEM((WIN + s, 2 * AUG), BF16),
            pltpu.VMEM((AUG, 3 * N_HEADS * AUG), BF16),
        ],
        compiler_params=_cparams(("parallel", "arbitrary")),
        name="nsa_attn",
    )(zq, zkv, kcv)


def _back_kernel(ya_ref, yb_ref, yc_ref, yd_ref, x_ref, wo_ref, gmix_ref, gpre_ref, wgu_ref, wd_ref,
                 gpost_ref, o_ref):
    y = jnp.concatenate([ya_ref[...], yb_ref[...], yc_ref[...], yd_ref[...]], axis=1)
    x = x_ref[...] + _rms(_dot(y, wo_ref[...]), gmix_ref[...])
    h = _rms(x, gpre_ref[...]).astype(BF16)
    f = jnp.zeros((TM, D_MODEL), F32)
    c0 = 0
    for width in FFN_CHUNKS:
        gate = _dot(h, wgu_ref[:, c0:c0 + width])
        up = _dot(h, wgu_ref[:, FFN_HIDDEN + c0:FFN_HIDDEN + c0 + width])
        act = ((gate * jax.nn.sigmoid(gate)) * up).astype(BF16)
        f = f + _dot(act, wd_ref[c0:c0 + width, :])
        c0 += width
    o_ref[...] = x + _rms(f, gpost_ref[...])


def _back(layer, ya, yb, yc, yd, x2, wo_all, gmix, gpre, wgu_all, wd_all, gpost):
    n = x2.shape[0]
    const = lambda shape: pl.BlockSpec(shape, lambda i: (0, 0), pipeline_mode=pl.Buffered(1))
    slab = lambda rows, cols: pl.BlockSpec((None, rows, cols), lambda i: (layer, 0, 0),
                                           pipeline_mode=pl.Buffered(1))
    yspec = pl.BlockSpec((TM, GW), lambda i: (i, 0))
    return pl.pallas_call(
        _back_kernel,
        grid=(n // TM,),
        in_specs=[
            yspec, yspec, yspec, yspec,
            pl.BlockSpec((TM, D_MODEL), lambda i: (i, 0)),
            slab(D_MODEL, D_MODEL), const((1, D_MODEL)), const((1, D_MODEL)),
            slab(D_MODEL, 2 * FFN_HIDDEN), slab(FFN_HIDDEN, D_MODEL),
            const((1, D_MODEL)),
        ],
        out_specs=pl.BlockSpec((TM, D_MODEL), lambda i: (i, 0)),
        out_shape=jax.ShapeDtypeStruct((n, D_MODEL), F32),
        compiler_params=_cparams(("parallel",)),
        name="back",
    )(ya, yb, yc, yd, x2, wo_all, gmix, gpre, wgu_all, wd_all, gpost)


def _interleave(wk, wv):
    z = jnp.zeros((CMP_BLOCK, HEAD_DIM, HEAD_DIM), wk.dtype)
    wk3 = wk.reshape(CMP_BLOCK, HEAD_DIM, HEAD_DIM)
    wv3 = wv.reshape(CMP_BLOCK, HEAD_DIM, HEAD_DIM)
    return jnp.concatenate([jnp.concatenate([wk3, z], axis=2), jnp.concatenate([z, wv3], axis=2)], axis=1)


def _block_diag(mats):
    n = len(mats)
    rows = []
    for a, m in enumerate(mats):
        rows.append(jnp.concatenate([m if a == c else jnp.zeros_like(m) for c in range(n)], axis=1))
    return jnp.concatenate(rows, axis=0)


def _layer(layer, x2, batch, seq, p, big):
    n = x2.shape[0]
    row = lambda v: v.reshape(1, -1)
    sg_w = jnp.transpose(p["sg_w"], (1, 0, 2)).reshape(SG_CHUNK, N_HEADS * SG_CHUNK)
    sg_bias = jnp.repeat(p["sg_b"].T, HEAD_DIM, axis=1)
    pool_w = _block_diag([p["pool_w"][gi] for gi in range(len(POOL_WINDOWS))]).astype(BF16)
    y_a, y_b, y_d, zq, zkv = _front(
        layer, x2.reshape(batch, seq, D_MODEL), row(p["g_pre_mix"]), big["w_in"],
        row(p["sg_ln_g"]), sg_w, sg_bias,
        p["cv_w"], row(p["cv_b"]), row(p["cv_ln_g"]), row(p["cv_ln_b"]), p["cv_pw"].astype(BF16),
        row(p["cv_pw_b"]), pool_w, row(p["pool_scale"]))

    pe = jnp.concatenate([p["cmp_pos_k"], p["cmp_pos_v"]], axis=1)
    w1 = _interleave(p["cmp_w1_k"], p["cmp_w1_v"]).astype(BF16)
    w2 = _block_diag([p["cmp_w2_k"], p["cmp_w2_v"]]).astype(BF16)
    kcv = _compress(zkv, pe, w1, w2)
    y_c = _nsa(zq, zkv, kcv)

    flat = lambda y: y.reshape(n, GW)
    return _back(layer, flat(y_a), flat(y_b), flat(y_c), flat(y_d), x2,
                 big["w_out"], row(p["g_post_mix"]), row(p["g_pre_ffn"]),
                 big["ffn_w_gu"], big["ffn_w_down"], row(p["g_post_ffn"]))


_PARAM_NAMES = ("g_pre_mix", "g_post_mix", "g_pre_ffn", "g_post_ffn", "w_in", "sg_ln_g", "sg_w", "sg_b",
                "cv_w", "cv_b", "cv_ln_g", "cv_ln_b", "cv_pw", "cv_pw_b", "cmp_pos_k", "cmp_pos_v",
                "cmp_w1_k", "cmp_w2_k", "cmp_w1_v", "cmp_w2_v", "pool_w", "pool_scale", "w_out",
                "ffn_w_gu", "ffn_w_down")


def kernel(x, g_pre_mix, g_post_mix, g_pre_ffn, g_post_ffn, w_in, sg_ln_g, sg_w, sg_b, cv_w, cv_b, cv_ln_g, cv_ln_b, cv_pw, cv_pw_b, cmp_pos_k, cmp_pos_v, cmp_w1_k, cmp_w2_k, cmp_w1_v, cmp_w2_v, pool_w, pool_scale, w_out, ffn_w_gu, ffn_w_down):
    params = dict(zip(_PARAM_NAMES, (g_pre_mix, g_post_mix, g_pre_ffn, g_post_ffn, w_in, sg_ln_g, sg_w,
                                     sg_b, cv_w, cv_b, cv_ln_g, cv_ln_b, cv_pw, cv_pw_b, cmp_pos_k,
                                     cmp_pos_v, cmp_w1_k, cmp_w2_k, cmp_w1_v, cmp_w2_v, pool_w,
                                     pool_scale, w_out, ffn_w_gu, ffn_w_down)))
    batch, seq, _ = x.shape
    n_layers = g_pre_mix.shape[0]
    w_in_all = jnp.concatenate(
        [w_in[:, :, :1280], w_in[:, :, 1676:1932], w_in[:, :, 1280:1676],
         jnp.zeros((n_layers, D_MODEL, Z_COLS - 1932), w_in.dtype)], axis=2).astype(BF16)
    big = {"w_in": w_in_all, "w_out": w_out.astype(BF16), "ffn_w_gu": ffn_w_gu.astype(BF16),
           "ffn_w_down": ffn_w_down.astype(BF16)}
    small = {k: v for k, v in params.items() if k not in big}
    x2 = x.reshape(batch * seq, D_MODEL)
    for layer in range(n_layers):
        x2 = _layer(layer, x2, batch, seq, {k: v[layer] for k, v in small.items()}, big)
    return x2.reshape(batch, seq, D_MODEL)
```

```python
import jax
import jax.numpy as jnp
from jax import lax
from jax.experimental import pallas as pl
from jax.experimental.pallas import tpu as pltpu

F32 = jnp.float32
BF16 = jnp.bfloat16

D_MODEL = 1024
GW = 256
HEAD_DIM = 64
N_HEADS = 4
SG_CHUNK = 128
CONV_WIDTH = 31
CMP_BLOCK = 32
CMP_STRIDE = 16
SLC_BLOCK = 64
SLC_TOPK = 8
WIN = 512
Q_BLOCK = 128
FORCE_BONUS = 1e4
NEG = -1e30
POOL_WINDOWS = (2, 4, 8, 16)
FFN_HIDDEN = 2816
RMS_EPS = 1e-6
LN_EPS = 1e-5
Z_COLS = 2048
Q_OFF = 1024
D_OFF = 1280
KV_OFF = 1536
KV_COLS = 512
N_CMP_PAD = 128
N_SLC = 32

TM = 512
TS = 512
CONV_HALO = 32
POOL_HALO = 16
SUB = 64
SUBLANES = 8
POOL_PAD = SUBLANES
SHIFT_CHUNK = 128
LOG2E = 1.4426950408889634
SLC_TK = 256
SLC_ROW_GROUPS = 2
SLC_SPLIT_KEYS = 768
AUG = 128
PAD_LANE = HEAD_DIM + N_SLC
MXU_DIM = 256
FFN_CHUNKS = (6 * MXU_DIM, 5 * MXU_DIM)
CMP_TB = 4
VMEM_LIMIT = 56 * 1024 * 1024


def _cparams(sem):
    return pltpu.CompilerParams(dimension_semantics=sem, vmem_limit_bytes=VMEM_LIMIT)


def _nt(a, b):
    return lax.dot_general(a, b, (((1,), (1,)), ((), ())), preferred_element_type=F32)


def _dot(a, b):
    return jnp.dot(a, b, preferred_element_type=F32)


def _rms(x, g):
    return (x * lax.rsqrt(jnp.mean(x * x, axis=-1, keepdims=True) + RMS_EPS)) * g


def _layernorm(x, g):
    mu = jnp.mean(x, axis=-1, keepdims=True)
    d = x - mu
    var = jnp.mean(d * d, axis=-1, keepdims=True)
    return (d * lax.rsqrt(var + LN_EPS)) * g


def _sgu_chunk(c, z_a, lng, w, bias, o_ref):
    lane_head = lax.broadcasted_iota(jnp.int32, (SG_CHUNK, GW), 1) // HEAD_DIM
    blk = z_a[c * SG_CHUNK:(c + 1) * SG_CHUNK, :]
    u = blk[:, :GW]
    vn = _layernorm(blk[:, GW:], lng).astype(BF16)
    zero = jnp.zeros_like(vn)
    v4 = jnp.concatenate([jnp.where(lane_head == h, vn, zero) for h in range(N_HEADS)], axis=0)
    sv = _dot(w, v4) + bias
    o_ref[0, c * SG_CHUNK:(c + 1) * SG_CHUNK, :] = (u * sv).astype(BF16)


def _conv_prep(si, z_b, hbuf):
    prev = hbuf[0, TS:TS + CONV_HALO, :]
    hbuf[0, 0:CONV_HALO, :] = jnp.where(si > 0, prev, 0.0)
    for r0 in range(0, TS, SUB):
        blk = z_b[r0:r0 + SUB, :]
        hbuf[0, CONV_HALO + r0:CONV_HALO + r0 + SUB, :] = blk[:, :GW] * jax.nn.sigmoid(blk[:, GW:])
    rows = CONV_HALO + TS - SUBLANES
    for r in range(1, SUBLANES):
        for j0 in range(0, rows, SHIFT_CHUNK):
            n = min(SHIFT_CHUNK, rows - j0)
            hbuf[r, j0:j0 + n, :] = hbuf[0, j0 + r:j0 + r + n, :]


def _conv_chunk(r0, cw, cb, lng, lnb, pw, pwb, o_ref, hbuf):
    lead = CONV_HALO - (CONV_WIDTH - 1)
    acc = jnp.zeros((SUB, GW), F32)
    for k in range(CONV_WIDTH):
        off = lead + k
        base = r0 + off - off % SUBLANES
        acc = acc + cw[k:k + 1, :] * hbuf[off % SUBLANES, base:base + SUB, :]
    y = _layernorm(acc + cb, lng) + lnb
    y = y * jax.nn.sigmoid(y)
    out = _dot(y.astype(BF16), pw) + pwb
    o_ref[0, r0:r0 + SUB, :] = out.astype(BF16)


def _pool_prep(si, z_d, xbuf, s2buf, s4buf, s8buf):
    data0 = POOL_PAD + POOL_HALO
    total = data0 + TS
    zeros = jnp.zeros((POOL_PAD, GW), F32)
    prev = xbuf[total - POOL_HALO:total, :]
    xbuf[0:POOL_PAD, :] = zeros
    s2buf[0:POOL_PAD, :] = zeros
    s4buf[0:POOL_PAD, :] = zeros
    xbuf[POOL_PAD:data0, :] = jnp.where(si > 0, prev, 0.0)
    xbuf[data0:total, :] = z_d
    for src, dst, shift in ((xbuf, s2buf, 1), (s2buf, s4buf, 2), (s4buf, s8buf, 4)):
        for j0 in range(POOL_PAD, total, SHIFT_CHUNK):
            n = min(SHIFT_CHUNK, total - j0)
            dst[j0:j0 + n, :] = src[j0:j0 + n, :] + src[j0 - shift:j0 - shift + n, :]


def _pool_chunk(si, r0, w, scale, o_ref, xbuf, s2buf, s4buf, s8buf):
    lane_grp = lax.broadcasted_iota(jnp.int32, (1, GW), 1) // (GW // len(POOL_WINDOWS))
    win = jnp.zeros((1, GW), jnp.int32)
    for gi, wlen in enumerate(POOL_WINDOWS):
        win = jnp.where(lane_grp == gi, wlen, win)
    j = POOL_PAD + POOL_HALO + r0
    x = xbuf[j:j + SUB, :]
    s8 = s8buf[j:j + SUB, :]
    s16 = s8 + s8buf[j - 8:j - 8 + SUB, :]
    acc = jnp.where(lane_grp == 0, s2buf[j:j + SUB, :],
                    jnp.where(lane_grp == 1, s4buf[j:j + SUB, :],
                              jnp.where(lane_grp == 2, s8, s16)))
    t1 = si * TS + r0 + 1 + lax.broadcasted_iota(jnp.int32, (SUB, GW), 0)
    cnt = jnp.minimum(t1, win).astype(F32)
    diff = acc / cnt - x
    out = _dot(diff.astype(BF16), w) * scale
    o_ref[0, r0:r0 + SUB, :] = out.astype(BF16)


def _front_kernel(x_ref, g_ref, w_ref,
                  sg_lng_ref, sg_w_ref, sg_bias_ref,
                  cw_ref, cb_ref, cv_lng_ref, cv_lnb_ref, pw_ref, pwb_ref,
                  pool_w_ref, pool_scale_ref,
                  ya_ref, yb_ref, yd_ref, zq_ref, zkv_ref,
                  hbuf, xbuf, s2buf, s4buf, s8buf):
    si = pl.program_id(1)

    @pl.when(si == 0)
    def _():
        hbuf[0, TS:TS + CONV_HALO, :] = jnp.zeros((CONV_HALO, GW), F32)
        xbuf[POOL_PAD + TS:POOL_PAD + POOL_HALO + TS, :] = jnp.zeros((POOL_HALO, GW), F32)

    conv_args = (cw_ref[...], cb_ref[...], cv_lng_ref[...], cv_lnb_ref[...], pw_ref[...], pwb_ref[...],
                 yb_ref, hbuf)
    pool_bufs = (xbuf, s2buf, s4buf, s8buf)
    conv = lambda k: _conv_chunk(k * SUB, *conv_args)
    pool = lambda k: _pool_chunk(si, k * SUB, pool_w_ref[...], pool_scale_ref[...], yd_ref, *pool_bufs)

    h = _rms(x_ref[0], g_ref[...]).astype(BF16)
    _conv_prep(si, _dot(h, w_ref[:, 2 * GW:4 * GW]), hbuf)
    z_qd = _dot(h, w_ref[:, Q_OFF:Q_OFF + 2 * GW])
    conv(0)
    conv(1)
    zq_ref[0] = z_qd[:, :GW]
    _pool_prep(si, z_qd[:, GW:], *pool_bufs)
    z_a = _dot(h, w_ref[:, 0:2 * GW])
    conv(2)
    conv(3)
    pool(0)
    pool(1)
    pool(2)
    pool(3)
    zkv_ref[0] = _dot(h, w_ref[:, KV_OFF:KV_OFF + KV_COLS])
    conv(4)
    conv(5)
    pool(4)
    pool(5)
    conv(6)
    conv(7)
    pool(6)
    pool(7)
    row = lax.broadcasted_iota(jnp.int32, (SG_CHUNK, 4 * SG_CHUNK), 0)
    col = lax.broadcasted_iota(jnp.int32, (SG_CHUNK, 4 * SG_CHUNK), 1) % SG_CHUNK
    sg_w = jnp.where(row >= col, sg_w_ref[...], 0.0).astype(BF16)
    for c in range(TS // SG_CHUNK):
        _sgu_chunk(c, z_a, sg_lng_ref[...], sg_w, sg_bias_ref[...], ya_ref)


def _front(layer, x3, g, w_all, sg_lng, sg_w, sg_bias, cw, cb, cv_lng, cv_lnb, pw, pwb, pool_w, pool_scale):
    b, s, _ = x3.shape
    const = lambda shape: pl.BlockSpec(shape, lambda bi, si: (0,) * len(shape))
    w_spec = pl.BlockSpec((None, D_MODEL, Z_COLS), lambda bi, si: (layer, 0, 0))
    tile = lambda width: pl.BlockSpec((1, TS, width), lambda bi, si: (bi, si, 0))
    pool_rows = POOL_PAD + POOL_HALO + TS
    return pl.pallas_call(
        _front_kernel,
        grid=(b, s // TS),
        in_specs=[
            tile(D_MODEL), const((1, D_MODEL)), w_spec,
            const((1, GW)), const((SG_CHUNK, 4 * SG_CHUNK)), const((SG_CHUNK, GW)),
            const((CONV_WIDTH, GW)), const((1, GW)), const((1, GW)), const((1, GW)),
            const((GW, GW)), const((1, GW)),
            const((GW, GW)), const((1, GW)),
        ],
        out_specs=[tile(GW), tile(GW), tile(GW), tile(GW), tile(KV_COLS)],
        out_shape=[
            jax.ShapeDtypeStruct((b, s, GW), BF16),
            jax.ShapeDtypeStruct((b, s, GW), BF16),
            jax.ShapeDtypeStruct((b, s, GW), BF16),
            jax.ShapeDtypeStruct((b, s, GW), F32),
            jax.ShapeDtypeStruct((b, s, KV_COLS), F32),
        ],
        scratch_shapes=[pltpu.VMEM((SUBLANES, CONV_HALO + TS, GW), F32)]
        + [pltpu.VMEM((pool_rows, GW), F32)] * 4,
        compiler_params=_cparams(("parallel", "arbitrary")),
        name="front",
    )(x3, g, w_all, sg_lng, sg_w, sg_bias, cw, cb, cv_lng, cv_lnb, pw, pwb, pool_w, pool_scale)


def _compress_kernel(x_ref, pe_ref, w1_ref, w2_ref, o_ref):
    m = CMP_TB * N_CMP_PAD
    p0 = jnp.zeros((m, 128), F32)
    p1 = jnp.zeros((m, 128), F32)
    for r in range(CMP_STRIDE):
        xr = jnp.concatenate([x_ref[bi, pl.ds(r, N_CMP_PAD, stride=CMP_STRIDE), :] for bi in range(CMP_TB)],
                             axis=0)
        p0 = p0 + _dot((xr + pe_ref[r:r + 1, :]).astype(BF16), w1_ref[r])
        p1 = p1 + _dot((xr + pe_ref[CMP_STRIDE + r:CMP_STRIDE + r + 1, :]).astype(BF16),
                       w1_ref[CMP_STRIDE + r])
    pre = p0 + pltpu.roll(p1, m - 1, axis=0)
    hid = pre * jax.nn.sigmoid(pre)
    out = _dot(hid.astype(BF16), w2_ref[...])
    rowid = lax.broadcasted_iota(jnp.int32, (m, 128), 0) % N_CMP_PAD
    out = jnp.where(rowid < N_CMP_PAD - 1, out, 0.0)
    o_ref[...] = out.reshape(CMP_TB, N_CMP_PAD, 128)


def _compress(zkv, pe, w1, w2):
    b, s, _ = zkv.shape
    return pl.pallas_call(
        _compress_kernel,
        grid=(b // CMP_TB,),
        in_specs=[
            pl.BlockSpec((CMP_TB, s, 128), lambda i: (i, 0, 0)),
            pl.BlockSpec((CMP_BLOCK, 128), lambda i: (0, 0)),
            pl.BlockSpec((CMP_BLOCK, 128, 128), lambda i: (0, 0, 0)),
            pl.BlockSpec((128, 128), lambda i: (0, 0)),
        ],
        out_specs=pl.BlockSpec((CMP_TB, N_CMP_PAD, 128), lambda i: (i, 0, 0)),
        out_shape=jax.ShapeDtypeStruct((b, N_CMP_PAD, 128), F32),
        compiler_params=_cparams(("parallel",)),
        name="nsa_compress",
    )(zkv, pe, w1, w2)


def _nsa_kernel(q_ref, kv_ref, kcv_ref, o_ref, ks_aug, kts, vs_aug, kw_aug, vw_aug, gsel):
    i = pl.program_id(1)
    seq = kv_ref.shape[1]
    nq = Q_BLOCK

    @pl.when(i == 0)
    def _():
        eye = jnp.where(lax.broadcasted_iota(jnp.int32, (AUG, AUG), 0)
                        == lax.broadcasted_iota(jnp.int32, (AUG, AUG), 1), 1.0, 0.0).astype(BF16)
        lane_p = lax.broadcasted_iota(jnp.int32, (WIN, AUG), 1)
        kw_aug[0:WIN, :] = jnp.where(lane_p == PAD_LANE, 1.0, 0.0).astype(BF16)
        vw_aug[0:WIN, :] = jnp.zeros((WIN, 2 * AUG), BF16)
        gsel[...] = jnp.where(lax.broadcasted_iota(jnp.int32, gsel.shape, 0)
                              == lax.broadcasted_iota(jnp.int32, gsel.shape, 1) // AUG, 1.0, 0.0).astype(BF16)
        lane = lax.broadcasted_iota(jnp.int32, (256, AUG), 1)
        ones = jnp.ones((256, AUG), BF16)
        for r in range(0, seq, 256):
            blk = kv_ref[0, r:r + 256, :]
            t_s = blk[:, 128:256]
            t_w = blk[:, 256:384]
            key_blk = (r + lax.broadcasted_iota(jnp.int32, (256, AUG), 0)) // SLC_BLOCK
            onehot = jnp.where(lane - HEAD_DIM == key_blk, 1.0, 0.0)
            ks_aug[r:r + 256, :] = jnp.where(lane < HEAD_DIM, t_s, onehot).astype(BF16)
            v_s = jnp.where(lane < HEAD_DIM, pltpu.roll(t_s, HEAD_DIM, axis=1), 0.0).astype(BF16)
            vs_aug[r:r + 256, :] = jnp.concatenate([v_s, ones], axis=1)
            kw_aug[WIN + r:WIN + r + 256, :] = jnp.where(lane < HEAD_DIM, t_w, 0.0).astype(BF16)
            v_w = jnp.where(lane < HEAD_DIM, pltpu.roll(t_w, HEAD_DIM, axis=1), 0.0).astype(BF16)
            vw_aug[WIN + r:WIN + r + 256, :] = jnp.concatenate([v_w, ones], axis=1)
        for c in range(0, seq, SLC_TK):
            kts[:, c:c + SLC_TK] = _nt(eye, ks_aug[c:c + SLC_TK, :]).astype(BF16)

    per = SLC_TK // nq
    n_dyn = (i + per - 1) // per
    for n_kt in range(seq // SLC_TK + 1):
        @pl.when(n_dyn == n_kt)
        def _(n_kt=n_kt):
            _nsa_tile(n_kt, i, q_ref, kv_ref, kcv_ref, o_ref, ks_aug, kts, vs_aug, kw_aug, vw_aug, gsel)


def _nsa_tile(n_kt, i, q_ref, kv_ref, kcv_ref, o_ref, ks_aug, kts, vs_aug, kw_aug, vw_aug, gsel):
    nq = Q_BLOCK
    hq = N_HEADS * nq
    eye = jnp.where(lax.broadcasted_iota(jnp.int32, (AUG, AUG), 0)
                    == lax.broadcasted_iota(jnp.int32, (AUG, AUG), 1), 1.0, 0.0).astype(BF16)
    q0 = pl.multiple_of(i * nq, nq)
    q = q_ref[0] * (HEAD_DIM ** -0.5 * LOG2E)
    lane_q = lax.broadcasted_iota(jnp.int32, (nq, AUG), 1)

    def stack_heads(extra):
        tiles = []
        for h in range(N_HEADS):
            t = q[:, AUG * (h // 2):AUG * (h // 2 + 1)]
            if h % 2:
                t = pltpu.roll(t, HEAD_DIM, axis=1)
            tiles.append(jnp.where(lane_q < HEAD_DIM, t, extra))
        return jnp.concatenate(tiles, axis=0).astype(BF16)

    pad_bias = jnp.where(lane_q == PAD_LANE, NEG, 0.0)
    qw = stack_heads(pad_bias)
    trow = q0 + lax.broadcasted_iota(jnp.int32, (nq, 1), 0)
    row_l = lax.broadcasted_iota(jnp.int32, (nq, nq), 0)
    col_l = lax.broadcasted_iota(jnp.int32, (nq, nq), 1)
    tri_le = (col_l <= row_l)[None]
    tri_gt = (col_l > row_l)[None]

    nk = WIN + nq
    sw = _nt(qw, kw_aug[pl.ds(q0, nk), :]).reshape(N_HEADS, nq, nk)
    sw = jnp.concatenate([jnp.where(tri_gt, sw[:, :, :nq], NEG), sw[:, :, nq:WIN],
                          jnp.where(tri_le, sw[:, :, WIN:], NEG)], axis=-1)
    pw = jnp.exp2(sw - jnp.max(sw, axis=-1, keepdims=True)).astype(BF16)
    o_win = _dot(pw.reshape(hq, nk), vw_aug[pl.ds(q0, nk), :])

    kcv = kcv_ref[0]
    lane_c = lax.broadcasted_iota(jnp.int32, (N_CMP_PAD, AUG), 1)
    kc = jnp.where(lane_c < HEAD_DIM, kcv, 0.0).astype(BF16)
    vc = jnp.where(lane_c < HEAD_DIM, pltpu.roll(kcv, HEAD_DIM, axis=1), 0.0).astype(BF16)
    sd = _nt(qw, ks_aug[pl.ds(q0, nq), :]).reshape(N_HEADS, nq, nq)
    sd = jnp.where(tri_le, sd, NEG).reshape(hq, nq)
    md = jnp.max(sd, axis=-1, keepdims=True)

    s = _nt(qw, kc).reshape(N_HEADS, nq, N_CMP_PAD)
    cidx = lax.broadcasted_iota(jnp.int32, (nq, N_CMP_PAD), 1)
    cmask = (cidx * CMP_STRIDE + (CMP_BLOCK - 1) <= trow) & (cidx < N_CMP_PAD - 1)
    sm = jnp.where(cmask[None], s, NEG)
    e = jnp.exp2(sm - jnp.max(sm, axis=-1, keepdims=True))
    p = e / jnp.sum(e, axis=-1, keepdims=True)
    p = jnp.where((trow >= CMP_BLOCK - 1)[None], p, 0.0)
    o_cmp = _dot(p.reshape(hq, N_CMP_PAD).astype(BF16), vc)

    psum = jnp.sum(p, axis=0)
    p_hi = psum.astype(BF16)
    p_lo = (psum - p_hi.astype(F32)).astype(BF16)
    jrow = lax.broadcasted_iota(jnp.int32, (N_SLC, N_CMP_PAD), 0)
    ccol = lax.broadcasted_iota(jnp.int32, (N_SLC, N_CMP_PAD), 1)
    ov = ((ccol * CMP_STRIDE <= jrow * SLC_BLOCK + (SLC_BLOCK - 1))
          & (ccol * CMP_STRIDE + (CMP_BLOCK - 1) >= jrow * SLC_BLOCK)
          & (ccol < N_CMP_PAD - 1))
    ov = jnp.where(ov, 1.0, 0.0).astype(BF16)
    imp = _nt(ov, p_hi) + _nt(ov, p_lo)
    jj = lax.broadcasted_iota(jnp.int32, (N_SLC, nq), 0)
    tt = q0 + lax.broadcasted_iota(jnp.int32, (N_SLC, nq), 1)
    cur = tt // SLC_BLOCK
    valid = jj * SLC_BLOCK <= tt
    forced = (jj == 0) | (jj == cur) | (jj == cur - 1)
    score = jnp.where(valid, imp + jnp.where(forced, FORCE_BONUS, 0.0), NEG)
    rank = jnp.zeros((N_SLC, nq), F32)
    for j2 in range(N_SLC):
        other = score[j2:j2 + 1, :]
        beats = (other > score) | ((other == score) & (jj > j2))
        rank = rank + jnp.where(beats, 1.0, 0.0)
    sel_t = jnp.where((rank < SLC_TOPK) & (score > NEG / 2), 1.0, 0.0)
    bias_t = jnp.where((sel_t > 0.5) & (jj < 2 * i), 0.0, NEG)
    bias_t = jnp.concatenate([jnp.zeros((HEAD_DIM, nq), F32), bias_t,
                              jnp.zeros((AUG - HEAD_DIM - N_SLC, nq), F32)], axis=0).astype(BF16)
    blk_bias = _nt(eye, bias_t)

    if n_kt:
        n_main = n_kt * SLC_TK
        qs = stack_heads(blk_bias + pad_bias)
        k_main = kts[:, 0:n_main]
        v_main = vs_aug[0:n_main, :]
        v_diag = vs_aug[pl.ds(q0, nq), :]
        n_grp = SLC_ROW_GROUPS if n_main >= SLC_SPLIT_KEYS else 1
        rows = hq // n_grp
        by_rows = lambda a, b: jnp.concatenate([_dot(a[r0:r0 + rows], b) for r0 in range(0, hq, rows)], axis=0)
        s_main = by_rows(qs, k_main)
        m = jnp.maximum(md, jnp.max(s_main, axis=-1, keepdims=True))
        p_main = jnp.exp2(s_main - m).astype(BF16)
        p_diag = jnp.exp2(sd - m).astype(BF16)
        o_slc = _dot(p_diag, v_diag) + by_rows(p_main, v_main)
    else:
        o_slc = _dot(jnp.exp2(sd - md).astype(BF16), vs_aug[pl.ds(q0, nq), :])

    g = jax.nn.sigmoid(kv_ref[0, pl.ds(q0, nq), 384:384 + AUG])
    g_hi = g.astype(BF16)
    g_lo = (g - g_hi.astype(F32)).astype(BF16)
    g_rep = _dot(g_hi, gsel[...]) + _dot(g_lo, gsel[...])
    gate = lambda br: jnp.concatenate(
        [g_rep[:, (3 * h + br) * AUG:(3 * h + br + 1) * AUG] for h in range(N_HEADS)], axis=0)
    o = (gate(0) * o_cmp
         + (gate(1) / o_slc[:, AUG:]) * o_slc[:, :AUG]
         + (gate(2) / o_win[:, AUG:]) * o_win[:, :AUG])
    halves = [jnp.where(lane_q < HEAD_DIM, o[2 * a * nq:(2 * a + 1) * nq, :],
                        pltpu.roll(o[(2 * a + 1) * nq:(2 * a + 2) * nq, :], HEAD_DIM, axis=1))
              for a in range(N_HEADS // 2)]
    o_ref[0] = jnp.concatenate(halves, axis=1).astype(BF16)


def _nsa(zq, zkv, kcv):
    b, s, _ = zq.shape
    return pl.pallas_call(
        _nsa_kernel,
        grid=(b, s // Q_BLOCK),
        in_specs=[
            pl.BlockSpec((1, Q_BLOCK, GW), lambda bi, qi: (bi, qi, 0)),
            pl.BlockSpec((1, s, KV_COLS), lambda bi, qi: (bi, 0, 0)),
            pl.BlockSpec((1, N_CMP_PAD, 128), lambda bi, qi: (bi, 0, 0)),
        ],
        out_specs=pl.BlockSpec((1, Q_BLOCK, GW), lambda bi, qi: (bi, qi, 0)),
        out_shape=jax.ShapeDtypeStruct((b, s, GW), BF16),
        scratch_shapes=[
            pltpu.VMEM((s, AUG), BF16),
            pltpu.VMEM((AUG, s), BF16),
            pltpu.VMEM((s, 2 * AUG), BF16),
            pltpu.VMEM((WIN + s, AUG), BF16),
            pltpu.VMEM((WIN + s, 2 * AUG), BF16),
            pltpu.VMEM((AUG, 3 * N_HEADS * AUG), BF16),
        ],
        compiler_params=_cparams(("parallel", "arbitrary")),
        name="nsa_attn",
    )(zq, zkv, kcv)


def _back_kernel(ya_ref, yb_ref, yc_ref, yd_ref, x_ref, wo_ref, gmix_ref, gpre_ref, wgu_ref, wd_ref,
                 gpost_ref, o_ref):
    xs, hs = [], []
    for r0 in range(0, TM, TM // 2):
        rows = slice(r0, r0 + TM // 2)
        y = jnp.concatenate([ya_ref[rows, :], yb_ref[rows, :], yc_ref[rows, :], yd_ref[rows, :]], axis=1)
        x_half = x_ref[rows, :] + _rms(_dot(y, wo_ref[...]), gmix_ref[...])
        xs.append(x_half)
        hs.append(_rms(x_half, gpre_ref[...]).astype(BF16))
    x = jnp.concatenate(xs, axis=0)
    h = jnp.concatenate(hs, axis=0)
    f = jnp.zeros((TM, D_MODEL), F32)
    c0 = 0
    for width in FFN_CHUNKS:
        gate = _dot(h, wgu_ref[:, c0:c0 + width])
        up = _dot(h, wgu_ref[:, FFN_HIDDEN + c0:FFN_HIDDEN + c0 + width])
        act = ((gate * jax.nn.sigmoid(gate)) * up).astype(BF16)
        f = f + _dot(act, wd_ref[c0:c0 + width, :])
        c0 += width
    o_ref[...] = x + _rms(f, gpost_ref[...])


def _back(layer, ya, yb, yc, yd, x2, wo_all, gmix, gpre, wgu_all, wd_all, gpost):
    n = x2.shape[0]
    const = lambda shape: pl.BlockSpec(shape, lambda i: (0, 0), pipeline_mode=pl.Buffered(1))
    slab = lambda rows, cols: pl.BlockSpec((None, rows, cols), lambda i: (layer, 0, 0),
                                           pipeline_mode=pl.Buffered(1))
    yspec = pl.BlockSpec((TM, GW), lambda i: (i, 0))
    return pl.pallas_call(
        _back_kernel,
        grid=(n // TM,),
        in_specs=[
            yspec, yspec, yspec, yspec,
            pl.BlockSpec((TM, D_MODEL), lambda i: (i, 0)),
            slab(D_MODEL, D_MODEL), const((1, D_MODEL)), const((1, D_MODEL)),
            slab(D_MODEL, 2 * FFN_HIDDEN), slab(FFN_HIDDEN, D_MODEL),
            const((1, D_MODEL)),
        ],
        out_specs=pl.BlockSpec((TM, D_MODEL), lambda i: (i, 0)),
        out_shape=jax.ShapeDtypeStruct((n, D_MODEL), F32),
        compiler_params=_cparams(("parallel",)),
        name="back",
    )(ya, yb, yc, yd, x2, wo_all, gmix, gpre, wgu_all, wd_all, gpost)


def _interleave(wk, wv):
    z = jnp.zeros((CMP_BLOCK, HEAD_DIM, HEAD_DIM), wk.dtype)
    wk3 = wk.reshape(CMP_BLOCK, HEAD_DIM, HEAD_DIM)
    wv3 = wv.reshape(CMP_BLOCK, HEAD_DIM, HEAD_DIM)
    return jnp.concatenate([jnp.concatenate([wk3, z], axis=2), jnp.concatenate([z, wv3], axis=2)], axis=1)


def _block_diag(mats):
    n = len(mats)
    rows = []
    for a, m in enumerate(mats):
        rows.append(jnp.concatenate([m if a == c else jnp.zeros_like(m) for c in range(n)], axis=1))
    return jnp.concatenate(rows, axis=0)


def _layer(layer, x2, batch, seq, p, big):
    n = x2.shape[0]
    row = lambda v: v.reshape(1, -1)
    sg_w = jnp.transpose(p["sg_w"], (1, 0, 2)).reshape(SG_CHUNK, N_HEADS * SG_CHUNK)
    sg_bias = jnp.repeat(p["sg_b"].T, HEAD_DIM, axis=1)
    pool_w = _block_diag([p["pool_w"][gi] for gi in range(len(POOL_WINDOWS))]).astype(BF16)
    y_a, y_b, y_d, zq, zkv = _front(
        layer, x2.reshape(batch, seq, D_MODEL), row(p["g_pre_mix"]), big["w_in"],
        row(p["sg_ln_g"]), sg_w, sg_bias,
        p["cv_w"], row(p["cv_b"]), row(p["cv_ln_g"]), row(p["cv_ln_b"]), p["cv_pw"].astype(BF16),
        row(p["cv_pw_b"]), pool_w, row(p["pool_scale"]))

    pe = jnp.concatenate([p["cmp_pos_k"], p["cmp_pos_v"]], axis=1)
    w1 = _interleave(p["cmp_w1_k"], p["cmp_w1_v"]).astype(BF16)
    w2 = _block_diag([p["cmp_w2_k"], p["cmp_w2_v"]]).astype(BF16)
    kcv = _compress(zkv, pe, w1, w2)
    y_c = _nsa(zq, zkv, kcv)

    flat = lambda y: y.reshape(n, GW)
    return _back(layer, flat(y_a), flat(y_b), flat(y_c), flat(y_d), x2,
                 big["w_out"], row(p["g_post_mix"]), row(p["g_pre_ffn"]),
                 big["ffn_w_gu"], big["ffn_w_down"], row(p["g_post_ffn"]))


_PARAM_NAMES = ("g_pre_mix", "g_post_mix", "g_pre_ffn", "g_post_ffn", "w_in", "sg_ln_g", "sg_w", "sg_b",
                "cv_w", "cv_b", "cv_ln_g", "cv_ln_b", "cv_pw", "cv_pw_b", "cmp_pos_k", "cmp_pos_v",
                "cmp_w1_k", "cmp_w2_k", "cmp_w1_v", "cmp_w2_v", "pool_w", "pool_scale", "w_out",
                "ffn_w_gu", "ffn_w_down")


def kernel(x, g_pre_mix, g_post_mix, g_pre_ffn, g_post_ffn, w_in, sg_ln_g, sg_w, sg_b, cv_w, cv_b, cv_ln_g, cv_ln_b, cv_pw, cv_pw_b, cmp_pos_k, cmp_pos_v, cmp_w1_k, cmp_w2_k, cmp_w1_v, cmp_w2_v, pool_w, pool_scale, w_out, ffn_w_gu, ffn_w_down):
    params = dict(zip(_PARAM_NAMES, (g_pre_mix, g_post_mix, g_pre_ffn, g_post_ffn, w_in, sg_ln_g, sg_w,
                                     sg_b, cv_w, cv_b, cv_ln_g, cv_ln_b, cv_pw, cv_pw_b, cmp_pos_k,
                                     cmp_pos_v, cmp_w1_k, cmp_w2_k, cmp_w1_v, cmp_w2_v, pool_w,
                                     pool_scale, w_out, ffn_w_gu, ffn_w_down)))
    batch, seq, _ = x.shape
    n_layers = g_pre_mix.shape[0]
    w_in_all = jnp.concatenate(
        [w_in[:, :, :1280], w_in[:, :, 1676:1932], w_in[:, :, 1280:1676],
         jnp.zeros((n_layers, D_MODEL, Z_COLS - 1932), w_in.dtype)], axis=2).astype(BF16)
    big = {"w_in": w_in_all, "w_out": w_out.astype(BF16), "ffn_w_gu": ffn_w_gu.astype(BF16),
           "ffn_w_down": ffn_w_down.astype(BF16)}
    small = {k: v for k, v in params.items() if k not in big}
    x2 = x.reshape(batch * seq, D_MODEL)
    for layer in range(n_layers):
        x2 = _layer(layer, x2, batch, seq, {k: v[layer] for k, v in small.items()}, big)
    return x2.reshape(batch, seq, D_MODEL)
```

```python
import jax
import jax.numpy as jnp
from jax import lax
from jax.experimental import pallas as pl
from jax.experimental.pallas import tpu as pltpu

F32 = jnp.float32
BF16 = jnp.bfloat16

D_MODEL = 1024
GW = 256
HEAD_DIM = 64
N_HEADS = 4
SG_CHUNK = 128
CONV_WIDTH = 31
CMP_BLOCK = 32
CMP_STRIDE = 16
SLC_BLOCK = 64
SLC_TOPK = 8
WIN = 512
Q_BLOCK = 128
FORCE_BONUS = 1e4
NEG = -1e30
POOL_WINDOWS = (2, 4, 8, 16)
FFN_HIDDEN = 2816
RMS_EPS = 1e-6
LN_EPS = 1e-5
Z_COLS = 2048
Q_OFF = 1024
D_OFF = 1280
KV_OFF = 1536
KV_COLS = 512
N_CMP_PAD = 128
N_SLC = 32

TM = 512
TS = 1024
CONV_HALO = 32
POOL_HALO = 16
SUB = 64
SUBLANES = 8
POOL_PAD = SUBLANES
SHIFT_CHUNK = 128
LOG2E = 1.4426950408889634
SLC_TK = 256
SLC_ROW_GROUPS = 2
SLC_SPLIT_KEYS = 768
AUG = 128
PAD_LANE = HEAD_DIM + N_SLC
MXU_DIM = 256
FFN_CHUNKS = (6 * MXU_DIM, 5 * MXU_DIM)
CMP_TB = 4
VMEM_LIMIT = 56 * 1024 * 1024


def _cparams(sem):
    return pltpu.CompilerParams(dimension_semantics=sem, vmem_limit_bytes=VMEM_LIMIT)


def _nt(a, b):
    return lax.dot_general(a, b, (((1,), (1,)), ((), ())), preferred_element_type=F32)


def _dot(a, b):
    return jnp.dot(a, b, preferred_element_type=F32)


def _rms(x, g):
    return (x * lax.rsqrt(jnp.mean(x * x, axis=-1, keepdims=True) + RMS_EPS)) * g


def _layernorm(x, g):
    mu = jnp.mean(x, axis=-1, keepdims=True)
    d = x - mu
    var = jnp.mean(d * d, axis=-1, keepdims=True)
    return (d * lax.rsqrt(var + LN_EPS)) * g


def _sgu_chunk(c, z_a, lng, w, bias, o_ref):
    lane_head = lax.broadcasted_iota(jnp.int32, (SG_CHUNK, GW), 1) // HEAD_DIM
    blk = z_a[c * SG_CHUNK:(c + 1) * SG_CHUNK, :]
    u = blk[:, :GW]
    vn = _layernorm(blk[:, GW:], lng).astype(BF16)
    zero = jnp.zeros_like(vn)
    v4 = jnp.concatenate([jnp.where(lane_head == h, vn, zero) for h in range(N_HEADS)], axis=0)
    sv = _dot(w, v4) + bias
    o_ref[0, c * SG_CHUNK:(c + 1) * SG_CHUNK, :] = (u * sv).astype(BF16)


def _conv_prep(si, z_b, hbuf):
    prev = hbuf[0, TS:TS + CONV_HALO, :]
    hbuf[0, 0:CONV_HALO, :] = jnp.where(si > 0, prev, 0.0)
    for r0 in range(0, TS, SUB):
        blk = z_b[r0:r0 + SUB, :]
        hbuf[0, CONV_HALO + r0:CONV_HALO + r0 + SUB, :] = blk[:, :GW] * jax.nn.sigmoid(blk[:, GW:])
    rows = CONV_HALO + TS - SUBLANES
    for r in range(1, SUBLANES):
        for j0 in range(0, rows, SHIFT_CHUNK):
            n = min(SHIFT_CHUNK, rows - j0)
            hbuf[r, j0:j0 + n, :] = hbuf[0, j0 + r:j0 + r + n, :]


def _conv_chunk(r0, cw, cb, lng, lnb, pw, pwb, o_ref, hbuf):
    lead = CONV_HALO - (CONV_WIDTH - 1)
    acc = jnp.zeros((SUB, GW), F32)
    for k in range(CONV_WIDTH):
        off = lead + k
        base = r0 + off - off % SUBLANES
        acc = acc + cw[k:k + 1, :] * hbuf[off % SUBLANES, base:base + SUB, :]
    y = _layernorm(acc + cb, lng) + lnb
    y = y * jax.nn.sigmoid(y)
    out = _dot(y.astype(BF16), pw) + pwb
    o_ref[0, r0:r0 + SUB, :] = out.astype(BF16)


def _pool_prep(si, z_d, xbuf, s2buf, s4buf, s8buf):
    data0 = POOL_PAD + POOL_HALO
    total = data0 + TS
    zeros = jnp.zeros((POOL_PAD, GW), F32)
    prev = xbuf[total - POOL_HALO:total, :]
    xbuf[0:POOL_PAD, :] = zeros
    s2buf[0:POOL_PAD, :] = zeros
    s4buf[0:POOL_PAD, :] = zeros
    xbuf[POOL_PAD:data0, :] = jnp.where(si > 0, prev, 0.0)
    xbuf[data0:total, :] = z_d
    for src, dst, shift in ((xbuf, s2buf, 1), (s2buf, s4buf, 2), (s4buf, s8buf, 4)):
        for j0 in range(POOL_PAD, total, SHIFT_CHUNK):
            n = min(SHIFT_CHUNK, total - j0)
            dst[j0:j0 + n, :] = src[j0:j0 + n, :] + src[j0 - shift:j0 - shift + n, :]


def _pool_chunk(si, r0, w, scale, o_ref, xbuf, s2buf, s4buf, s8buf):
    lane_grp = lax.broadcasted_iota(jnp.int32, (1, GW), 1) // (GW // len(POOL_WINDOWS))
    win = jnp.zeros((1, GW), jnp.int32)
    for gi, wlen in enumerate(POOL_WINDOWS):
        win = jnp.where(lane_grp == gi, wlen, win)
    j = POOL_PAD + POOL_HALO + r0
    x = xbuf[j:j + SUB, :]
    s8 = s8buf[j:j + SUB, :]
    s16 = s8 + s8buf[j - 8:j - 8 + SUB, :]
    acc = jnp.where(lane_grp == 0, s2buf[j:j + SUB, :],
                    jnp.where(lane_grp == 1, s4buf[j:j + SUB, :],
                              jnp.where(lane_grp == 2, s8, s16)))
    t1 = si * TS + r0 + 1 + lax.broadcasted_iota(jnp.int32, (SUB, GW), 0)
    cnt = jnp.minimum(t1, win).astype(F32)
    diff = acc / cnt - x
    out = _dot(diff.astype(BF16), w) * scale
    o_ref[0, r0:r0 + SUB, :] = out.astype(BF16)


def _front_kernel(x_ref, g_ref, w_ref,
                  sg_lng_ref, sg_w_ref, sg_bias_ref,
                  cw_ref, cb_ref, cv_lng_ref, cv_lnb_ref, pw_ref, pwb_ref,
                  pool_w_ref, pool_scale_ref,
                  ya_ref, yb_ref, yd_ref, zq_ref, zkv_ref,
                  hbuf, xbuf, s2buf, s4buf, s8buf):
    si = pl.program_id(1)

    @pl.when(si == 0)
    def _():
        hbuf[0, TS:TS + CONV_HALO, :] = jnp.zeros((CONV_HALO, GW), F32)
        xbuf[POOL_PAD + TS:POOL_PAD + POOL_HALO + TS, :] = jnp.zeros((POOL_HALO, GW), F32)

    conv_args = (cw_ref[...], cb_ref[...], cv_lng_ref[...], cv_lnb_ref[...], pw_ref[...], pwb_ref[...],
                 yb_ref, hbuf)
    pool_bufs = (xbuf, s2buf, s4buf, s8buf)
    conv = lambda k: _conv_chunk(k * SUB, *conv_args)
    pool = lambda k: _pool_chunk(si, k * SUB, pool_w_ref[...], pool_scale_ref[...], yd_ref, *pool_bufs)

    h = _rms(x_ref[0], g_ref[...]).astype(BF16)
    quarter = TS // SUB // 4
    _conv_prep(si, _dot(h, w_ref[:, 2 * GW:4 * GW]), hbuf)
    z_qd = _dot(h, w_ref[:, Q_OFF:Q_OFF + 2 * GW])
    for k in range(0, quarter):
        conv(k)
    zq_ref[0] = z_qd[:, :GW]
    _pool_prep(si, z_qd[:, GW:], *pool_bufs)
    z_a = _dot(h, w_ref[:, 0:2 * GW])
    for k in range(quarter, 2 * quarter):
        conv(k)
    for k in range(0, 2 * quarter):
        pool(k)
    zkv_ref[0] = _dot(h, w_ref[:, KV_OFF:KV_OFF + KV_COLS])
    for q in (2, 3):
        for k in range(q * quarter, (q + 1) * quarter):
            conv(k)
        for k in range(q * quarter, (q + 1) * quarter):
            pool(k)
    row = lax.broadcasted_iota(jnp.int32, (SG_CHUNK, 4 * SG_CHUNK), 0)
    col = lax.broadcasted_iota(jnp.int32, (SG_CHUNK, 4 * SG_CHUNK), 1) % SG_CHUNK
    sg_w = jnp.where(row >= col, sg_w_ref[...], 0.0).astype(BF16)
    for c in range(TS // SG_CHUNK):
        _sgu_chunk(c, z_a, sg_lng_ref[...], sg_w, sg_bias_ref[...], ya_ref)


def _front(layer, x3, g, w_all, sg_lng, sg_w, sg_bias, cw, cb, cv_lng, cv_lnb, pw, pwb, pool_w, pool_scale):
    b, s, _ = x3.shape
    const = lambda shape: pl.BlockSpec(shape, lambda bi, si: (0,) * len(shape))
    w_spec = pl.BlockSpec((None, D_MODEL, Z_COLS), lambda bi, si: (layer, 0, 0))
    tile = lambda width: pl.BlockSpec((1, TS, width), lambda bi, si: (bi, si, 0))
    pool_rows = POOL_PAD + POOL_HALO + TS
    return pl.pallas_call(
        _front_kernel,
        grid=(b, s // TS),
        in_specs=[
            tile(D_MODEL), const((1, D_MODEL)), w_spec,
            const((1, GW)), const((SG_CHUNK, 4 * SG_CHUNK)), const((SG_CHUNK, GW)),
            const((CONV_WIDTH, GW)), const((1, GW)), const((1, GW)), const((1, GW)),
            const((GW, GW)), const((1, GW)),
            const((GW, GW)), const((1, GW)),
        ],
        out_specs=[tile(GW), tile(GW), tile(GW), tile(GW), tile(KV_COLS)],
        out_shape=[
            jax.ShapeDtypeStruct((b, s, GW), BF16),
            jax.ShapeDtypeStruct((b, s, GW), BF16),
            jax.ShapeDtypeStruct((b, s, GW), BF16),
            jax.ShapeDtypeStruct((b, s, GW), F32),
            jax.ShapeDtypeStruct((b, s, KV_COLS), F32),
        ],
        scratch_shapes=[pltpu.VMEM((SUBLANES, CONV_HALO + TS, GW), F32)]
        + [pltpu.VMEM((pool_rows, GW), F32)] * 4,
        compiler_params=_cparams(("parallel", "arbitrary")),
        name="front",
    )(x3, g, w_all, sg_lng, sg_w, sg_bias, cw, cb, cv_lng, cv_lnb, pw, pwb, pool_w, pool_scale)


def _compress_kernel(x_ref, pe_ref, w1_ref, w2_ref, o_ref):
    m = CMP_TB * N_CMP_PAD
    p0 = jnp.zeros((m, 128), F32)
    p1 = jnp.zeros((m, 128), F32)
    for r in range(CMP_STRIDE):
        xr = jnp.concatenate([x_ref[bi, pl.ds(r, N_CMP_PAD, stride=CMP_STRIDE), :] for bi in range(CMP_TB)],
                             axis=0)
        p0 = p0 + _dot((xr + pe_ref[r:r + 1, :]).astype(BF16), w1_ref[r])
        p1 = p1 + _dot((xr + pe_ref[CMP_STRIDE + r:CMP_STRIDE + r + 1, :]).astype(BF16),
                       w1_ref[CMP_STRIDE + r])
    pre = p0 + pltpu.roll(p1, m - 1, axis=0)
    hid = pre * jax.nn.sigmoid(pre)
    out = _dot(hid.astype(BF16), w2_ref[...])
    rowid = lax.broadcasted_iota(jnp.int32, (m, 128), 0) % N_CMP_PAD
    out = jnp.where(rowid < N_CMP_PAD - 1, out, 0.0)
    o_ref[...] = out.reshape(CMP_TB, N_CMP_PAD, 128)


def _compress(zkv, pe, w1, w2):
    b, s, _ = zkv.shape
    return pl.pallas_call(
        _compress_kernel,
        grid=(b // CMP_TB,),
        in_specs=[
            pl.BlockSpec((CMP_TB, s, 128), lambda i: (i, 0, 0)),
            pl.BlockSpec((CMP_BLOCK, 128), lambda i: (0, 0)),
            pl.BlockSpec((CMP_BLOCK, 128, 128), lambda i: (0, 0, 0)),
            pl.BlockSpec((128, 128), lambda i: (0, 0)),
        ],
        out_specs=pl.BlockSpec((CMP_TB, N_CMP_PAD, 128), lambda i: (i, 0, 0)),
        out_shape=jax.ShapeDtypeStruct((b, N_CMP_PAD, 128), F32),
        compiler_params=_cparams(("parallel",)),
        name="nsa_compress",
    )(zkv, pe, w1, w2)


def _nsa_kernel(q_ref, kv_ref, kcv_ref, o_ref, ks_aug, kts, vs_aug, kw_aug, vw_aug, gsel):
    i = pl.program_id(1)
    seq = kv_ref.shape[1]
    nq = Q_BLOCK

    @pl.when(i == 0)
    def _():
        eye = jnp.where(lax.broadcasted_iota(jnp.int32, (AUG, AUG), 0)
                        == lax.broadcasted_iota(jnp.int32, (AUG, AUG), 1), 1.0, 0.0).astype(BF16)
        lane_p = lax.broadcasted_iota(jnp.int32, (WIN, AUG), 1)
        kw_aug[0:WIN, :] = jnp.where(lane_p == PAD_LANE, 1.0, 0.0).astype(BF16)
        vw_aug[0:WIN, :] = jnp.zeros((WIN, 2 * AUG), BF16)
        gsel[...] = jnp.where(lax.broadcasted_iota(jnp.int32, gsel.shape, 0)
                              == lax.broadcasted_iota(jnp.int32, gsel.shape, 1) // AUG, 1.0, 0.0).astype(BF16)
        lane = lax.broadcasted_iota(jnp.int32, (256, AUG), 1)
        ones = jnp.ones((256, AUG), BF16)
        for r in range(0, seq, 256):
            blk = kv_ref[0, r:r + 256, :]
            t_s = blk[:, 128:256]
            t_w = blk[:, 256:384]
            key_blk = (r + lax.broadcasted_iota(jnp.int32, (256, AUG), 0)) // SLC_BLOCK
            onehot = jnp.where(lane - HEAD_DIM == key_blk, 1.0, 0.0)
            ks_aug[r:r + 256, :] = jnp.where(lane < HEAD_DIM, t_s, onehot).astype(BF16)
            v_s = jnp.where(lane < HEAD_DIM, pltpu.roll(t_s, HEAD_DIM, axis=1), 0.0).astype(BF16)
            vs_aug[r:r + 256, :] = jnp.concatenate([v_s, ones], axis=1)
            kw_aug[WIN + r:WIN + r + 256, :] = jnp.where(lane < HEAD_DIM, t_w, 0.0).astype(BF16)
            v_w = jnp.where(lane < HEAD_DIM, pltpu.roll(t_w, HEAD_DIM, axis=1), 0.0).astype(BF16)
            vw_aug[WIN + r:WIN + r + 256, :] = jnp.concatenate([v_w, ones], axis=1)
        for c in range(0, seq, SLC_TK):
            kts[:, c:c + SLC_TK] = _nt(eye, ks_aug[c:c + SLC_TK, :]).astype(BF16)

    per = SLC_TK // nq
    n_dyn = (i + per - 1) // per
    for n_kt in range(seq // SLC_TK + 1):
        @pl.when(n_dyn == n_kt)
        def _(n_kt=n_kt):
            _nsa_tile(n_kt, i, q_ref, kv_ref, kcv_ref, o_ref, ks_aug, kts, vs_aug, kw_aug, vw_aug, gsel)


def _nsa_tile(n_kt, i, q_ref, kv_ref, kcv_ref, o_ref, ks_aug, kts, vs_aug, kw_aug, vw_aug, gsel):
    nq = Q_BLOCK
    hq = N_HEADS * nq
    eye = jnp.where(lax.broadcasted_iota(jnp.int32, (AUG, AUG), 0)
                    == lax.broadcasted_iota(jnp.int32, (AUG, AUG), 1), 1.0, 0.0).astype(BF16)
    q0 = pl.multiple_of(i * nq, nq)
    q = q_ref[0] * (HEAD_DIM ** -0.5 * LOG2E)
    lane_q = lax.broadcasted_iota(jnp.int32, (nq, AUG), 1)

    def stack_heads(extra):
        tiles = []
        for h in range(N_HEADS):
            t = q[:, AUG * (h // 2):AUG * (h // 2 + 1)]
            if h % 2:
                t = pltpu.roll(t, HEAD_DIM, axis=1)
            tiles.append(jnp.where(lane_q < HEAD_DIM, t, extra))
        return jnp.concatenate(tiles, axis=0).astype(BF16)

    pad_bias = jnp.where(lane_q == PAD_LANE, NEG, 0.0)
    qw = stack_heads(pad_bias)
    trow = q0 + lax.broadcasted_iota(jnp.int32, (nq, 1), 0)
    row_l = lax.broadcasted_iota(jnp.int32, (nq, nq), 0)
    col_l = lax.broadcasted_iota(jnp.int32, (nq, nq), 1)
    tri_le = (col_l <= row_l)[None]
    tri_gt = (col_l > row_l)[None]

    nk = WIN + nq
    sw = _nt(qw, kw_aug[pl.ds(q0, nk), :]).reshape(N_HEADS, nq, nk)
    sw = jnp.concatenate([jnp.where(tri_gt, sw[:, :, :nq], NEG), sw[:, :, nq:WIN],
                          jnp.where(tri_le, sw[:, :, WIN:], NEG)], axis=-1)
    pw = jnp.exp2(sw - jnp.max(sw, axis=-1, keepdims=True)).astype(BF16)
    o_win = _dot(pw.reshape(hq, nk), vw_aug[pl.ds(q0, nk), :])

    kcv = kcv_ref[0]
    lane_c = lax.broadcasted_iota(jnp.int32, (N_CMP_PAD, AUG), 1)
    kc = jnp.where(lane_c < HEAD_DIM, kcv, 0.0).astype(BF16)
    vc = jnp.where(lane_c < HEAD_DIM, pltpu.roll(kcv, HEAD_DIM, axis=1), 0.0).astype(BF16)
    sd = _nt(qw, ks_aug[pl.ds(q0, nq), :]).reshape(N_HEADS, nq, nq)
    sd = jnp.where(tri_le, sd, NEG).reshape(hq, nq)
    md = jnp.max(sd, axis=-1, keepdims=True)

    s = _nt(qw, kc).reshape(N_HEADS, nq, N_CMP_PAD)
    cidx = lax.broadcasted_iota(jnp.int32, (nq, N_CMP_PAD), 1)
    cmask = (cidx * CMP_STRIDE + (CMP_BLOCK - 1) <= trow) & (cidx < N_CMP_PAD - 1)
    sm = jnp.where(cmask[None], s, NEG)
    e = jnp.exp2(sm - jnp.max(sm, axis=-1, keepdims=True))
    p = e / jnp.sum(e, axis=-1, keepdims=True)
    p = jnp.where((trow >= CMP_BLOCK - 1)[None], p, 0.0)
    o_cmp = _dot(p.reshape(hq, N_CMP_PAD).astype(BF16), vc)

    psum = jnp.sum(p, axis=0)
    p_hi = psum.astype(BF16)
    p_lo = (psum - p_hi.astype(F32)).astype(BF16)
    jrow = lax.broadcasted_iota(jnp.int32, (N_SLC, N_CMP_PAD), 0)
    ccol = lax.broadcasted_iota(jnp.int32, (N_SLC, N_CMP_PAD), 1)
    ov = ((ccol * CMP_STRIDE <= jrow * SLC_BLOCK + (SLC_BLOCK - 1))
          & (ccol * CMP_STRIDE + (CMP_BLOCK - 1) >= jrow * SLC_BLOCK)
          & (ccol < N_CMP_PAD - 1))
    ov = jnp.where(ov, 1.0, 0.0).astype(BF16)
    imp = _nt(ov, p_hi) + _nt(ov, p_lo)
    jj = lax.broadcasted_iota(jnp.int32, (N_SLC, nq), 0)
    tt = q0 + lax.broadcasted_iota(jnp.int32, (N_SLC, nq), 1)
    cur = tt // SLC_BLOCK
    valid = jj * SLC_BLOCK <= tt
    forced = (jj == 0) | (jj == cur) | (jj == cur - 1)
    score = jnp.where(valid, imp + jnp.where(forced, FORCE_BONUS, 0.0), NEG)
    rank = jnp.zeros((N_SLC, nq), F32)
    for j2 in range(N_SLC):
        other = score[j2:j2 + 1, :]
        beats = (other > score) | ((other == score) & (jj > j2))
        rank = rank + jnp.where(beats, 1.0, 0.0)
    sel_t = jnp.where((rank < SLC_TOPK) & (score > NEG / 2), 1.0, 0.0)
    bias_t = jnp.where((sel_t > 0.5) & (jj < 2 * i), 0.0, NEG)
    bias_t = jnp.concatenate([jnp.zeros((HEAD_DIM, nq), F32), bias_t,
                              jnp.zeros((AUG - HEAD_DIM - N_SLC, nq), F32)], axis=0).astype(BF16)
    blk_bias = _nt(eye, bias_t)

    if n_kt:
        n_main = n_kt * SLC_TK
        qs = stack_heads(blk_bias + pad_bias)
        k_main = kts[:, 0:n_main]
        v_main = vs_aug[0:n_main, :]
        v_diag = vs_aug[pl.ds(q0, nq), :]
        n_grp = SLC_ROW_GROUPS if n_main >= SLC_SPLIT_KEYS else 1
        rows = hq // n_grp
        by_rows = lambda a, b: jnp.concatenate([_dot(a[r0:r0 + rows], b) for r0 in range(0, hq, rows)], axis=0)
        s_main = by_rows(qs, k_main)
        m = jnp.maximum(md, jnp.max(s_main, axis=-1, keepdims=True))
        p_main = jnp.exp2(s_main - m).astype(BF16)
        p_diag = jnp.exp2(sd - m).astype(BF16)
        o_slc = _dot(p_diag, v_diag) + by_rows(p_main, v_main)
    else:
        o_slc = _dot(jnp.exp2(sd - md).astype(BF16), vs_aug[pl.ds(q0, nq), :])

    g = jax.nn.sigmoid(kv_ref[0, pl.ds(q0, nq), 384:384 + AUG])
    g_hi = g.astype(BF16)
    g_lo = (g - g_hi.astype(F32)).astype(BF16)
    g_rep = _dot(g_hi, gsel[...]) + _dot(g_lo, gsel[...])
    gate = lambda br: jnp.concatenate(
        [g_rep[:, (3 * h + br) * AUG:(3 * h + br + 1) * AUG] for h in range(N_HEADS)], axis=0)
    o = (gate(0) * o_cmp
         + (gate(1) / o_slc[:, AUG:]) * o_slc[:, :AUG]
         + (gate(2) / o_win[:, AUG:]) * o_win[:, :AUG])
    halves = [jnp.where(lane_q < HEAD_DIM, o[2 * a * nq:(2 * a + 1) * nq, :],
                        pltpu.roll(o[(2 * a + 1) * nq:(2 * a + 2) * nq, :], HEAD_DIM, axis=1))
              for a in range(N_HEADS // 2)]
    o_ref[0] = jnp.concatenate(halves, axis=1).astype(BF16)


def _nsa(zq, zkv, kcv):
    b, s, _ = zq.shape
    return pl.pallas_call(
        _nsa_kernel,
        grid=(b, s // Q_BLOCK),
        in_specs=[
            pl.BlockSpec((1, Q_BLOCK, GW), lambda bi, qi: (bi, qi, 0)),
            pl.BlockSpec((1, s, KV_COLS), lambda bi, qi: (bi, 0, 0)),
            pl.BlockSpec((1, N_CMP_PAD, 128), lambda bi, qi: (bi, 0, 0)),
        ],
        out_specs=pl.BlockSpec((1, Q_BLOCK, GW), lambda bi, qi: (bi, qi, 0)),
        out_shape=jax.ShapeDtypeStruct((b, s, GW), BF16),
        scratch_shapes=[
            pltpu.VMEM((s, AUG), BF16),
            pltpu.VMEM((AUG, s), BF16),
            pltpu.VMEM((s, 2 * AUG), BF16),
            pltpu.VMEM((WIN + s, AUG), BF16),
            pltpu.VMEM((WIN + s, 2 * AUG), BF16),
            pltpu.VMEM((AUG, 3 * N_HEADS * AUG), BF16),
        ],
        compiler_params=_cparams(("parallel", "arbitrary")),
        name="nsa_attn",
    )(zq, zkv, kcv)


def _back_kernel(ya_ref, yb_ref, yc_ref, yd_ref, x_ref, wo_ref, gmix_ref, gpre_ref, wgu_ref, wd_ref,
                 gpost_ref, o_ref):
    xs, hs = [], []
    for r0 in range(0, TM, TM // 2):
        rows = slice(r0, r0 + TM // 2)
        y = jnp.concatenate([ya_ref[rows, :], yb_ref[rows, :], yc_ref[rows, :], yd_ref[rows, :]], axis=1)
        x_half = x_ref[rows, :] + _rms(_dot(y, wo_ref[...]), gmix_ref[...])
        xs.append(x_half)
        hs.append(_rms(x_half, gpre_ref[...]).astype(BF16))
    x = jnp.concatenate(xs, axis=0)
    h = jnp.concatenate(hs, axis=0)
    f = jnp.zeros((TM, D_MODEL), F32)
    c0 = 0
    for width in FFN_CHUNKS:
        gate = _dot(h, wgu_ref[:, c0:c0 + width])
        up = _dot(h, wgu_ref[:, FFN_HIDDEN + c0:FFN_HIDDEN + c0 + width])
        act = ((gate * jax.nn.sigmoid(gate)) * up).astype(BF16)
        f = f + _dot(act, wd_ref[c0:c0 + width, :])
        c0 += width
    o_ref[...] = x + _rms(f, gpost_ref[...])


def _back(layer, ya, yb, yc, yd, x2, wo_all, gmix, gpre, wgu_all, wd_all, gpost):
    n = x2.shape[0]
    const = lambda shape: pl.BlockSpec(shape, lambda i: (0, 0), pipeline_mode=pl.Buffered(1))
    slab = lambda rows, cols: pl.BlockSpec((None, rows, cols), lambda i: (layer, 0, 0),
                                           pipeline_mode=pl.Buffered(1))
    yspec = pl.BlockSpec((TM, GW), lambda i: (i, 0))
    return pl.pallas_call(
        _back_kernel,
        grid=(n // TM,),
        in_specs=[
            yspec, yspec, yspec, yspec,
            pl.BlockSpec((TM, D_MODEL), lambda i: (i, 0)),
            slab(D_MODEL, D_MODEL), const((1, D_MODEL)), const((1, D_MODEL)),
            slab(D_MODEL, 2 * FFN_HIDDEN), slab(FFN_HIDDEN, D_MODEL),
            const((1, D_MODEL)),
        ],
        out_specs=pl.BlockSpec((TM, D_MODEL), lambda i: (i, 0)),
        out_shape=jax.ShapeDtypeStruct((n, D_MODEL), F32),
        compiler_params=_cparams(("parallel",)),
        name="back",
    )(ya, yb, yc, yd, x2, wo_all, gmix, gpre, wgu_all, wd_all, gpost)


def _interleave(wk, wv):
    z = jnp.zeros((CMP_BLOCK, HEAD_DIM, HEAD_DIM), wk.dtype)
    wk3 = wk.reshape(CMP_BLOCK, HEAD_DIM, HEAD_DIM)
    wv3 = wv.reshape(CMP_BLOCK, HEAD_DIM, HEAD_DIM)
    return jnp.concatenate([jnp.concatenate([wk3, z], axis=2), jnp.concatenate([z, wv3], axis=2)], axis=1)


def _block_diag(mats):
    n = len(mats)
    rows = []
    for a, m in enumerate(mats):
        rows.append(jnp.concatenate([m if a == c else jnp.zeros_like(m) for c in range(n)], axis=1))
    return jnp.concatenate(rows, axis=0)


def _layer(layer, x2, batch, seq, p, big):
    n = x2.shape[0]
    row = lambda v: v.reshape(1, -1)
    sg_w = jnp.transpose(p["sg_w"], (1, 0, 2)).reshape(SG_CHUNK, N_HEADS * SG_CHUNK)
    sg_bias = jnp.repeat(p["sg_b"].T, HEAD_DIM, axis=1)
    pool_w = _block_diag([p["pool_w"][gi] for gi in range(len(POOL_WINDOWS))]).astype(BF16)
    y_a, y_b, y_d, zq, zkv = _front(
        layer, x2.reshape(batch, seq, D_MODEL), row(p["g_pre_mix"]), big["w_in"],
        row(p["sg_ln_g"]), sg_w, sg_bias,
        p["cv_w"], row(p["cv_b"]), row(p["cv_ln_g"]), row(p["cv_ln_b"]), p["cv_pw"].astype(BF16),
        row(p["cv_pw_b"]), pool_w, row(p["pool_scale"]))

    pe = jnp.concatenate([p["cmp_pos_k"], p["cmp_pos_v"]], axis=1)
    w1 = _interleave(p["cmp_w1_k"], p["cmp_w1_v"]).astype(BF16)
    w2 = _block_diag([p["cmp_w2_k"], p["cmp_w2_v"]]).astype(BF16)
    kcv = _compress(zkv, pe, w1, w2)
    y_c = _nsa(zq, zkv, kcv)

    flat = lambda y: y.reshape(n, GW)
    return _back(layer, flat(y_a), flat(y_b), flat(y_c), flat(y_d), x2,
                 big["w_out"], row(p["g_post_mix"]), row(p["g_pre_ffn"]),
                 big["ffn_w_gu"], big["ffn_w_down"], row(p["g_post_ffn"]))


_PARAM_NAMES = ("g_pre_mix", "g_post_mix", "g_pre_ffn", "g_post_ffn", "w_in", "sg_ln_g", "sg_w", "sg_b",
                "cv_w", "cv_b", "cv_ln_g", "cv_ln_b", "cv_pw", "cv_pw_b", "cmp_pos_k", "cmp_pos_v",
                "cmp_w1_k", "cmp_w2_k", "cmp_w1_v", "cmp_w2_v", "pool_w", "pool_scale", "w_out",
                "ffn_w_gu", "ffn_w_down")


def kernel(x, g_pre_mix, g_post_mix, g_pre_ffn, g_post_ffn, w_in, sg_ln_g, sg_w, sg_b, cv_w, cv_b, cv_ln_g, cv_ln_b, cv_pw, cv_pw_b, cmp_pos_k, cmp_pos_v, cmp_w1_k, cmp_w2_k, cmp_w1_v, cmp_w2_v, pool_w, pool_scale, w_out, ffn_w_gu, ffn_w_down):
    params = dict(zip(_PARAM_NAMES, (g_pre_mix, g_post_mix, g_pre_ffn, g_post_ffn, w_in, sg_ln_g, sg_w,
                                     sg_b, cv_w, cv_b, cv_ln_g, cv_ln_b, cv_pw, cv_pw_b, cmp_pos_k,
                                     cmp_pos_v, cmp_w1_k, cmp_w2_k, cmp_w1_v, cmp_w2_v, pool_w,
                                     pool_scale, w_out, ffn_w_gu, ffn_w_down)))
    batch, seq, _ = x.shape
    n_layers = g_pre_mix.shape[0]
    w_in_all = jnp.concatenate(
        [w_in[:, :, :1280], w_in[:, :, 1676:1932], w_in[:, :, 1280:1676],
         jnp.zeros((n_layers, D_MODEL, Z_COLS - 1932), w_in.dtype)], axis=2).astype(BF16)
    big = {"w_in": w_in_all, "w_out": w_out.astype(BF16), "ffn_w_gu": ffn_w_gu.astype(BF16),
           "ffn_w_down": ffn_w_down.astype(BF16)}
    small = {k: v for k, v in params.items() if k not in big}
    x2 = x.reshape(batch * seq, D_MODEL)
    for layer in range(n_layers):
        x2 = _layer(layer, x2, batch, seq, {k: v[layer] for k, v in small.items()}, big)
    return x2.reshape(batch, seq, D_MODEL)
```

```python
import jax
import jax.numpy as jnp
from jax import lax
from jax.experimental import pallas as pl
from jax.experimental.pallas import tpu as pltpu

F32 = jnp.float32
BF16 = jnp.bfloat16

D_MODEL = 1024
GW = 256
HEAD_DIM = 64
N_HEADS = 4
SG_CHUNK = 128
CONV_WIDTH = 31
CMP_BLOCK = 32
CMP_STRIDE = 16
SLC_BLOCK = 64
SLC_TOPK = 8
WIN = 512
Q_BLOCK = 128
FORCE_BONUS = 1e4
NEG = -1e30
POOL_WINDOWS = (2, 4, 8, 16)
FFN_HIDDEN = 2816
RMS_EPS = 1e-6
LN_EPS = 1e-5
Z_COLS = 2048
Q_OFF = 1024
D_OFF = 1280
KV_OFF = 1536
KV_COLS = 512
N_CMP_PAD = 128
N_SLC = 32

TM = 512
TS = 1024
CONV_HALO = 32
POOL_HALO = 16
SUB = 64
SUBLANES = 8
POOL_PAD = SUBLANES
SHIFT_CHUNK = 128
LOG2E = 1.4426950408889634
SLC_TK = 256
SLC_ROW_GROUPS = 2
SLC_SPLIT_KEYS = 768
AUG = 128
PAD_LANE = HEAD_DIM + N_SLC
MXU_DIM = 256
FFN_CHUNKS = (6 * MXU_DIM, 5 * MXU_DIM)
CMP_TB = 4
VMEM_LIMIT = 56 * 1024 * 1024


def _cparams(sem):
    return pltpu.CompilerParams(dimension_semantics=sem, vmem_limit_bytes=VMEM_LIMIT)


def _nt(a, b):
    return lax.dot_general(a, b, (((1,), (1,)), ((), ())), preferred_element_type=F32)


def _dot(a, b):
    return jnp.dot(a, b, preferred_element_type=F32)


def _rms(x, g):
    return (x * lax.rsqrt(jnp.mean(x * x, axis=-1, keepdims=True) + RMS_EPS)) * g


def _layernorm(x, g):
    mu = jnp.mean(x, axis=-1, keepdims=True)
    d = x - mu
    var = jnp.mean(d * d, axis=-1, keepdims=True)
    return (d * lax.rsqrt(var + LN_EPS)) * g


def _sgu_chunk(c, z_a, lng, w, bias, o_ref):
    lane_head = lax.broadcasted_iota(jnp.int32, (SG_CHUNK, GW), 1) // HEAD_DIM
    blk = z_a[c * SG_CHUNK:(c + 1) * SG_CHUNK, :]
    u = blk[:, :GW]
    vn = _layernorm(blk[:, GW:], lng).astype(BF16)
    zero = jnp.zeros_like(vn)
    v4 = jnp.concatenate([jnp.where(lane_head == h, vn, zero) for h in range(N_HEADS)], axis=0)
    sv = _dot(w, v4) + bias
    o_ref[0, c * SG_CHUNK:(c + 1) * SG_CHUNK, :] = (u * sv).astype(BF16)


def _conv_prep(si, z_b, hbuf):
    prev = hbuf[0, TS:TS + CONV_HALO, :]
    hbuf[0, 0:CONV_HALO, :] = jnp.where(si > 0, prev, 0.0)
    for r0 in range(0, TS, SUB):
        blk = z_b[r0:r0 + SUB, :]
        hbuf[0, CONV_HALO + r0:CONV_HALO + r0 + SUB, :] = blk[:, :GW] * jax.nn.sigmoid(blk[:, GW:])
    rows = CONV_HALO + TS - SUBLANES
    for r in range(1, SUBLANES):
        for j0 in range(0, rows, SHIFT_CHUNK):
            n = min(SHIFT_CHUNK, rows - j0)
            hbuf[r, j0:j0 + n, :] = hbuf[0, j0 + r:j0 + r + n, :]


def _conv_chunk(r0, cw, cb, lng, lnb, pw, pwb, o_ref, hbuf):
    lead = CONV_HALO - (CONV_WIDTH - 1)
    acc = jnp.zeros((SUB, GW), F32)
    for k in range(CONV_WIDTH):
        off = lead + k
        base = r0 + off - off % SUBLANES
        acc = acc + cw[k:k + 1, :] * hbuf[off % SUBLANES, base:base + SUB, :]
    y = _layernorm(acc + cb, lng) + lnb
    y = y * jax.nn.sigmoid(y)
    out = _dot(y.astype(BF16), pw) + pwb
    o_ref[0, r0:r0 + SUB, :] = out.astype(BF16)


def _pool_prep(si, z_d, xbuf, s2buf, s4buf, s8buf):
    data0 = POOL_PAD + POOL_HALO
    total = data0 + TS
    zeros = jnp.zeros((POOL_PAD, GW), F32)
    prev = xbuf[total - POOL_HALO:total, :]
    xbuf[0:POOL_PAD, :] = zeros
    s2buf[0:POOL_PAD, :] = zeros
    s4buf[0:POOL_PAD, :] = zeros
    xbuf[POOL_PAD:data0, :] = jnp.where(si > 0, prev, 0.0)
    xbuf[data0:total, :] = z_d
    for src, dst, shift in ((xbuf, s2buf, 1), (s2buf, s4buf, 2), (s4buf, s8buf, 4)):
        for j0 in range(POOL_PAD, total, SHIFT_CHUNK):
            n = min(SHIFT_CHUNK, total - j0)
            dst[j0:j0 + n, :] = src[j0:j0 + n, :] + src[j0 - shift:j0 - shift + n, :]


def _pool_chunk(si, r0, w, scale, o_ref, xbuf, s2buf, s4buf, s8buf):
    lane_grp = lax.broadcasted_iota(jnp.int32, (1, GW), 1) // (GW // len(POOL_WINDOWS))
    win = jnp.zeros((1, GW), jnp.int32)
    for gi, wlen in enumerate(POOL_WINDOWS):
        win = jnp.where(lane_grp == gi, wlen, win)
    j = POOL_PAD + POOL_HALO + r0
    x = xbuf[j:j + SUB, :]
    s8 = s8buf[j:j + SUB, :]
    s16 = s8 + s8buf[j - 8:j - 8 + SUB, :]
    acc = jnp.where(lane_grp == 0, s2buf[j:j + SUB, :],
                    jnp.where(lane_grp == 1, s4buf[j:j + SUB, :],
                              jnp.where(lane_grp == 2, s8, s16)))
    t1 = si * TS + r0 + 1 + lax.broadcasted_iota(jnp.int32, (SUB, GW), 0)
    cnt = jnp.minimum(t1, win).astype(F32)
    diff = acc / cnt - x
    out = _dot(diff.astype(BF16), w) * scale
    o_ref[0, r0:r0 + SUB, :] = out.astype(BF16)


def _front_kernel(x_ref, g_ref, w_ref,
                  sg_lng_ref, sg_w_ref, sg_bias_ref,
                  cw_ref, cb_ref, cv_lng_ref, cv_lnb_ref, pw_ref, pwb_ref,
                  pool_w_ref, pool_scale_ref,
                  ya_ref, yb_ref, yd_ref, zq_ref, zkv_ref,
                  hbuf, xbuf, s2buf, s4buf, s8buf):
    si = pl.program_id(1)

    @pl.when(si == 0)
    def _():
        hbuf[0, TS:TS + CONV_HALO, :] = jnp.zeros((CONV_HALO, GW), F32)
        xbuf[POOL_PAD + TS:POOL_PAD + POOL_HALO + TS, :] = jnp.zeros((POOL_HALO, GW), F32)

    conv_args = (cw_ref[...], cb_ref[...], cv_lng_ref[...], cv_lnb_ref[...], pw_ref[...], pwb_ref[...],
                 yb_ref, hbuf)
    pool_bufs = (xbuf, s2buf, s4buf, s8buf)
    conv = lambda k: _conv_chunk(k * SUB, *conv_args)
    pool = lambda k: _pool_chunk(si, k * SUB, pool_w_ref[...], pool_scale_ref[...], yd_ref, *pool_bufs)

    h = _rms(x_ref[0], g_ref[...]).astype(BF16)
    quarter = TS // SUB // 4
    _conv_prep(si, _dot(h, w_ref[:, 2 * GW:4 * GW]), hbuf)
    z_qd = _dot(h, w_ref[:, Q_OFF:Q_OFF + 2 * GW])
    for k in range(0, quarter):
        conv(k)
    zq_ref[0] = z_qd[:, :GW]
    _pool_prep(si, z_qd[:, GW:], *pool_bufs)
    z_a = _dot(h, w_ref[:, 0:2 * GW])
    for k in range(quarter, 2 * quarter):
        conv(k)
    for k in range(0, 2 * quarter):
        pool(k)
    zkv_ref[0] = _dot(h, w_ref[:, KV_OFF:KV_OFF + KV_COLS])
    for q in (2, 3):
        for k in range(q * quarter, (q + 1) * quarter):
            conv(k)
        for k in range(q * quarter, (q + 1) * quarter):
            pool(k)
    row = lax.broadcasted_iota(jnp.int32, (SG_CHUNK, 4 * SG_CHUNK), 0)
    col = lax.broadcasted_iota(jnp.int32, (SG_CHUNK, 4 * SG_CHUNK), 1) % SG_CHUNK
    sg_w = jnp.where(row >= col, sg_w_ref[...], 0.0).astype(BF16)
    for c in range(TS // SG_CHUNK):
        _sgu_chunk(c, z_a, sg_lng_ref[...], sg_w, sg_bias_ref[...], ya_ref)


def _layer_spec(arr, layer, **kw):
    tail = arr.shape[1:]
    return pl.BlockSpec((None,) + tail, lambda *_: (layer,) + (0,) * len(tail), **kw)


_FRONT_PARAMS = ("g_pre_mix", "w_in", "sg_ln_g", "sg_w", "sg_bias", "cv_w", "cv_b", "cv_ln_g", "cv_ln_b",
                 "cv_pw", "cv_pw_b", "pool_w", "pool_scale")


def _front(layer, x3, prm):
    b, s, _ = x3.shape
    tile = lambda width: pl.BlockSpec((1, TS, width), lambda bi, si: (bi, si, 0))
    pool_rows = POOL_PAD + POOL_HALO + TS
    weights = [prm[k] for k in _FRONT_PARAMS]
    return pl.pallas_call(
        _front_kernel,
        grid=(b, s // TS),
        in_specs=[tile(D_MODEL)] + [_layer_spec(w, layer) for w in weights],
        out_specs=[tile(GW), tile(GW), tile(GW), tile(GW), tile(KV_COLS)],
        out_shape=[
            jax.ShapeDtypeStruct((b, s, GW), BF16),
            jax.ShapeDtypeStruct((b, s, GW), BF16),
            jax.ShapeDtypeStruct((b, s, GW), BF16),
            jax.ShapeDtypeStruct((b, s, GW), F32),
            jax.ShapeDtypeStruct((b, s, KV_COLS), F32),
        ],
        scratch_shapes=[pltpu.VMEM((SUBLANES, CONV_HALO + TS, GW), F32)]
        + [pltpu.VMEM((pool_rows, GW), F32)] * 4,
        compiler_params=_cparams(("parallel", "arbitrary")),
        name="front",
    )(x3, *weights)


def _compress_kernel(x_ref, pe_ref, w1_ref, w2_ref, o_ref):
    m = CMP_TB * N_CMP_PAD
    p0 = jnp.zeros((m, 128), F32)
    p1 = jnp.zeros((m, 128), F32)
    for r in range(CMP_STRIDE):
        xr = jnp.concatenate([x_ref[bi, pl.ds(r, N_CMP_PAD, stride=CMP_STRIDE), :] for bi in range(CMP_TB)],
                             axis=0)
        p0 = p0 + _dot((xr + pe_ref[r:r + 1, :]).astype(BF16), w1_ref[r])
        p1 = p1 + _dot((xr + pe_ref[CMP_STRIDE + r:CMP_STRIDE + r + 1, :]).astype(BF16),
                       w1_ref[CMP_STRIDE + r])
    pre = p0 + pltpu.roll(p1, m - 1, axis=0)
    hid = pre * jax.nn.sigmoid(pre)
    out = _dot(hid.astype(BF16), w2_ref[...])
    rowid = lax.broadcasted_iota(jnp.int32, (m, 128), 0) % N_CMP_PAD
    out = jnp.where(rowid < N_CMP_PAD - 1, out, 0.0)
    o_ref[...] = out.reshape(CMP_TB, N_CMP_PAD, 128)


def _compress(layer, zkv, prm):
    b, s, _ = zkv.shape
    weights = [prm[k] for k in ("cmp_pe", "cmp_w1", "cmp_w2")]
    return pl.pallas_call(
        _compress_kernel,
        grid=(b // CMP_TB,),
        in_specs=[pl.BlockSpec((CMP_TB, s, 128), lambda i: (i, 0, 0))] + [_layer_spec(w, layer) for w in weights],
        out_specs=pl.BlockSpec((CMP_TB, N_CMP_PAD, 128), lambda i: (i, 0, 0)),
        out_shape=jax.ShapeDtypeStruct((b, N_CMP_PAD, 128), F32),
        compiler_params=_cparams(("parallel",)),
        name="nsa_compress",
    )(zkv, *weights)


def _nsa_kernel(q_ref, kv_ref, kcv_ref, o_ref, ks_aug, kts, vs_aug, kw_aug, vw_aug, gsel):
    i = pl.program_id(1)
    seq = kv_ref.shape[1]
    nq = Q_BLOCK

    @pl.when(i == 0)
    def _():
        eye = jnp.where(lax.broadcasted_iota(jnp.int32, (AUG, AUG), 0)
                        == lax.broadcasted_iota(jnp.int32, (AUG, AUG), 1), 1.0, 0.0).astype(BF16)
        lane_p = lax.broadcasted_iota(jnp.int32, (WIN, AUG), 1)
        kw_aug[0:WIN, :] = jnp.where(lane_p == PAD_LANE, 1.0, 0.0).astype(BF16)
        vw_aug[0:WIN, :] = jnp.zeros((WIN, 2 * AUG), BF16)
        gsel[...] = jnp.where(lax.broadcasted_iota(jnp.int32, gsel.shape, 0)
                              == lax.broadcasted_iota(jnp.int32, gsel.shape, 1) // AUG, 1.0, 0.0).astype(BF16)
        lane = lax.broadcasted_iota(jnp.int32, (256, AUG), 1)
        ones = jnp.ones((256, AUG), BF16)
        for r in range(0, seq, 256):
            blk = kv_ref[0, r:r + 256, :]
            t_s = blk[:, 128:256]
            t_w = blk[:, 256:384]
            key_blk = (r + lax.broadcasted_iota(jnp.int32, (256, AUG), 0)) // SLC_BLOCK
            onehot = jnp.where(lane - HEAD_DIM == key_blk, 1.0, 0.0)
            ks_aug[r:r + 256, :] = jnp.where(lane < HEAD_DIM, t_s, onehot).astype(BF16)
            v_s = jnp.where(lane < HEAD_DIM, pltpu.roll(t_s, HEAD_DIM, axis=1), 0.0).astype(BF16)
            vs_aug[r:r + 256, :] = jnp.concatenate([v_s, ones], axis=1)
            kw_aug[WIN + r:WIN + r + 256, :] = jnp.where(lane < HEAD_DIM, t_w, 0.0).astype(BF16)
            v_w = jnp.where(lane < HEAD_DIM, pltpu.roll(t_w, HEAD_DIM, axis=1), 0.0).astype(BF16)
            vw_aug[WIN + r:WIN + r + 256, :] = jnp.concatenate([v_w, ones], axis=1)
        for c in range(0, seq, SLC_TK):
            kts[:, c:c + SLC_TK] = _nt(eye, ks_aug[c:c + SLC_TK, :]).astype(BF16)

    per = SLC_TK // nq
    n_dyn = (i + per - 1) // per
    for n_kt in range(seq // SLC_TK + 1):
        @pl.when(n_dyn == n_kt)
        def _(n_kt=n_kt):
            _nsa_tile(n_kt, i, q_ref, kv_ref, kcv_ref, o_ref, ks_aug, kts, vs_aug, kw_aug, vw_aug, gsel)


def _nsa_tile(n_kt, i, q_ref, kv_ref, kcv_ref, o_ref, ks_aug, kts, vs_aug, kw_aug, vw_aug, gsel):
    nq = Q_BLOCK
    hq = N_HEADS * nq
    eye = jnp.where(lax.broadcasted_iota(jnp.int32, (AUG, AUG), 0)
                    == lax.broadcasted_iota(jnp.int32, (AUG, AUG), 1), 1.0, 0.0).astype(BF16)
    q0 = pl.multiple_of(i * nq, nq)
    q = q_ref[0] * (HEAD_DIM ** -0.5 * LOG2E)
    lane_q = lax.broadcasted_iota(jnp.int32, (nq, AUG), 1)

    def stack_heads(extra):
        tiles = []
        for h in range(N_HEADS):
            t = q[:, AUG * (h // 2):AUG * (h // 2 + 1)]
            if h % 2:
                t = pltpu.roll(t, HEAD_DIM, axis=1)
            tiles.append(jnp.where(lane_q < HEAD_DIM, t, extra))
        return jnp.concatenate(tiles, axis=0).astype(BF16)

    pad_bias = jnp.where(lane_q == PAD_LANE, NEG, 0.0)
    qw = stack_heads(pad_bias)
    trow = q0 + lax.broadcasted_iota(jnp.int32, (nq, 1), 0)
    row_l = lax.broadcasted_iota(jnp.int32, (nq, nq), 0)
    col_l = lax.broadcasted_iota(jnp.int32, (nq, nq), 1)
    tri_le = (col_l <= row_l)[None]
    tri_gt = (col_l > row_l)[None]

    nk = WIN + nq
    sw = _nt(qw, kw_aug[pl.ds(q0, nk), :]).reshape(N_HEADS, nq, nk)
    sw = jnp.concatenate([jnp.where(tri_gt, sw[:, :, :nq], NEG), sw[:, :, nq:WIN],
                          jnp.where(tri_le, sw[:, :, WIN:], NEG)], axis=-1)
    pw = jnp.exp2(sw - jnp.max(sw, axis=-1, keepdims=True)).astype(BF16)
    o_win = _dot(pw.reshape(hq, nk), vw_aug[pl.ds(q0, nk), :])

    kcv = kcv_ref[0]
    lane_c = lax.broadcasted_iota(jnp.int32, (N_CMP_PAD, AUG), 1)
    kc = jnp.where(lane_c < HEAD_DIM, kcv, 0.0).astype(BF16)
    vc = jnp.where(lane_c < HEAD_DIM, pltpu.roll(kcv, HEAD_DIM, axis=1), 0.0).astype(BF16)
    sd = _nt(qw, ks_aug[pl.ds(q0, nq), :]).reshape(N_HEADS, nq, nq)
    sd = jnp.where(tri_le, sd, NEG).reshape(hq, nq)
    md = jnp.max(sd, axis=-1, keepdims=True)

    s = _nt(qw, kc).reshape(N_HEADS, nq, N_CMP_PAD)
    cidx = lax.broadcasted_iota(jnp.int32, (nq, N_CMP_PAD), 1)
    cmask = (cidx * CMP_STRIDE + (CMP_BLOCK - 1) <= trow) & (cidx < N_CMP_PAD - 1)
    sm = jnp.where(cmask[None], s, NEG)
    e = jnp.exp2(sm - jnp.max(sm, axis=-1, keepdims=True))
    p = e / jnp.sum(e, axis=-1, keepdims=True)
    p = jnp.where((trow >= CMP_BLOCK - 1)[None], p, 0.0)
    o_cmp = _dot(p.reshape(hq, N_CMP_PAD).astype(BF16), vc)

    psum = jnp.sum(p, axis=0)
    p_hi = psum.astype(BF16)
    p_lo = (psum - p_hi.astype(F32)).astype(BF16)
    jrow = lax.broadcasted_iota(jnp.int32, (N_SLC, N_CMP_PAD), 0)
    ccol = lax.broadcasted_iota(jnp.int32, (N_SLC, N_CMP_PAD), 1)
    ov = ((ccol * CMP_STRIDE <= jrow * SLC_BLOCK + (SLC_BLOCK - 1))
          & (ccol * CMP_STRIDE + (CMP_BLOCK - 1) >= jrow * SLC_BLOCK)
          & (ccol < N_CMP_PAD - 1))
    ov = jnp.where(ov, 1.0, 0.0).astype(BF16)
    imp = _nt(ov, p_hi) + _nt(ov, p_lo)
    jj = lax.broadcasted_iota(jnp.int32, (N_SLC, nq), 0)
    tt = q0 + lax.broadcasted_iota(jnp.int32, (N_SLC, nq), 1)
    cur = tt // SLC_BLOCK
    valid = jj * SLC_BLOCK <= tt
    forced = (jj == 0) | (jj == cur) | (jj == cur - 1)
    score = jnp.where(valid, imp + jnp.where(forced, FORCE_BONUS, 0.0), NEG)
    rank = jnp.zeros((N_SLC, nq), F32)
    for j2 in range(N_SLC):
        other = score[j2:j2 + 1, :]
        beats = (other > score) | ((other == score) & (jj > j2))
        rank = rank + jnp.where(beats, 1.0, 0.0)
    sel_t = jnp.where((rank < SLC_TOPK) & (score > NEG / 2), 1.0, 0.0)
    bias_t = jnp.where((sel_t > 0.5) & (jj < 2 * i), 0.0, NEG)
    bias_t = jnp.concatenate([jnp.zeros((HEAD_DIM, nq), F32), bias_t,
                              jnp.zeros((AUG - HEAD_DIM - N_SLC, nq), F32)], axis=0).astype(BF16)
    blk_bias = _nt(eye, bias_t)

    if n_kt:
        n_main = n_kt * SLC_TK
        qs = stack_heads(blk_bias + pad_bias)
        k_main = kts[:, 0:n_main]
        v_main = vs_aug[0:n_main, :]
        v_diag = vs_aug[pl.ds(q0, nq), :]
        n_grp = SLC_ROW_GROUPS if n_main >= SLC_SPLIT_KEYS else 1
        rows = hq // n_grp
        by_rows = lambda a, b: jnp.concatenate([_dot(a[r0:r0 + rows], b) for r0 in range(0, hq, rows)], axis=0)
        s_main = by_rows(qs, k_main)
        m = jnp.maximum(md, jnp.max(s_main, axis=-1, keepdims=True))
        p_main = jnp.exp2(s_main - m).astype(BF16)
        p_diag = jnp.exp2(sd - m).astype(BF16)
        o_slc = _dot(p_diag, v_diag) + by_rows(p_main, v_main)
    else:
        o_slc = _dot(jnp.exp2(sd - md).astype(BF16), vs_aug[pl.ds(q0, nq), :])

    g = jax.nn.sigmoid(kv_ref[0, pl.ds(q0, nq), 384:384 + AUG])
    g_hi = g.astype(BF16)
    g_lo = (g - g_hi.astype(F32)).astype(BF16)
    g_rep = _dot(g_hi, gsel[...]) + _dot(g_lo, gsel[...])
    gate = lambda br: jnp.concatenate(
        [g_rep[:, (3 * h + br) * AUG:(3 * h + br + 1) * AUG] for h in range(N_HEADS)], axis=0)
    o = (gate(0) * o_cmp
         + (gate(1) / o_slc[:, AUG:]) * o_slc[:, :AUG]
         + (gate(2) / o_win[:, AUG:]) * o_win[:, :AUG])
    halves = [jnp.where(lane_q < HEAD_DIM, o[2 * a * nq:(2 * a + 1) * nq, :],
                        pltpu.roll(o[(2 * a + 1) * nq:(2 * a + 2) * nq, :], HEAD_DIM, axis=1))
              for a in range(N_HEADS // 2)]
    o_ref[0] = jnp.concatenate(halves, axis=1).astype(BF16)


def _nsa(zq, zkv, kcv):
    b, s, _ = zq.shape
    return pl.pallas_call(
        _nsa_kernel,
        grid=(b, s // Q_BLOCK),
        in_specs=[
            pl.BlockSpec((1, Q_BLOCK, GW), lambda bi, qi: (bi, qi, 0)),
            pl.BlockSpec((1, s, KV_COLS), lambda bi, qi: (bi, 0, 0)),
            pl.BlockSpec((1, N_CMP_PAD, 128), lambda bi, qi: (bi, 0, 0)),
        ],
        out_specs=pl.BlockSpec((1, Q_BLOCK, GW), lambda bi, qi: (bi, qi, 0)),
        out_shape=jax.ShapeDtypeStruct((b, s, GW), BF16),
        scratch_shapes=[
            pltpu.VMEM((s, AUG), BF16),
            pltpu.VMEM((AUG, s), BF16),
            pltpu.VMEM((s, 2 * AUG), BF16),
            pltpu.VMEM((WIN + s, AUG), BF16),
            pltpu.VMEM((WIN + s, 2 * AUG), BF16),
            pltpu.VMEM((AUG, 3 * N_HEADS * AUG), BF16),
        ],
        compiler_params=_cparams(("parallel", "arbitrary")),
        name="nsa_attn",
    )(zq, zkv, kcv)


def _back_kernel(ya_ref, yb_ref, yc_ref, yd_ref, x_ref, wo_ref, gmix_ref, gpre_ref, wgu_ref, wd_ref,
                 gpost_ref, o_ref):
    xs, hs = [], []
    for r0 in range(0, TM, TM // 2):
        rows = slice(r0, r0 + TM // 2)
        y = jnp.concatenate([ya_ref[rows, :], yb_ref[rows, :], yc_ref[rows, :], yd_ref[rows, :]], axis=1)
        x_half = x_ref[rows, :] + _rms(_dot(y, wo_ref[...]), gmix_ref[...])
        xs.append(x_half)
        hs.append(_rms(x_half, gpre_ref[...]).astype(BF16))
    x = jnp.concatenate(xs, axis=0)
    h = jnp.concatenate(hs, axis=0)
    f = jnp.zeros((TM, D_MODEL), F32)
    c0 = 0
    for width in FFN_CHUNKS:
        gate = _dot(h, wgu_ref[:, c0:c0 + width])
        up = _dot(h, wgu_ref[:, FFN_HIDDEN + c0:FFN_HIDDEN + c0 + width])
        act = ((gate * jax.nn.sigmoid(gate)) * up).astype(BF16)
        f = f + _dot(act, wd_ref[c0:c0 + width, :])
        c0 += width
    o_ref[...] = x + _rms(f, gpost_ref[...])


_BACK_PARAMS = ("w_out", "g_post_mix", "g_pre_ffn", "ffn_w_gu", "ffn_w_down", "g_post_ffn")


def _back(layer, ya, yb, yc, yd, x2, prm):
    n = x2.shape[0]
    yspec = pl.BlockSpec((TM, GW), lambda i: (i, 0))
    weights = [prm[k] for k in _BACK_PARAMS]
    return pl.pallas_call(
        _back_kernel,
        grid=(n // TM,),
        in_specs=[yspec, yspec, yspec, yspec, pl.BlockSpec((TM, D_MODEL), lambda i: (i, 0))]
        + [_layer_spec(w, layer, pipeline_mode=pl.Buffered(1)) for w in weights],
        out_specs=pl.BlockSpec((TM, D_MODEL), lambda i: (i, 0)),
        out_shape=jax.ShapeDtypeStruct((n, D_MODEL), F32),
        compiler_params=_cparams(("parallel",)),
        name="back",
    )(ya, yb, yc, yd, x2, *weights)


def _interleave(wk, wv):
    n_layers = wk.shape[0]
    z = jnp.zeros((n_layers, CMP_BLOCK, HEAD_DIM, HEAD_DIM), wk.dtype)
    wk4 = wk.reshape(n_layers, CMP_BLOCK, HEAD_DIM, HEAD_DIM)
    wv4 = wv.reshape(n_layers, CMP_BLOCK, HEAD_DIM, HEAD_DIM)
    return jnp.concatenate([jnp.concatenate([wk4, z], axis=3), jnp.concatenate([z, wv4], axis=3)], axis=2)


def _block_diag(mats):
    n = len(mats)
    rows = []
    for a, m in enumerate(mats):
        rows.append(jnp.concatenate([m if a == c else jnp.zeros_like(m) for c in range(n)], axis=2))
    return jnp.concatenate(rows, axis=1)


def _prepare(p):
    n_layers = p["w_in"].shape[0]
    row = lambda v: v.reshape(n_layers, 1, -1)
    w_in = p["w_in"]
    w_in = jnp.concatenate(
        [w_in[:, :, :1280], w_in[:, :, 1676:1932], w_in[:, :, 1280:1676],
         jnp.zeros((n_layers, D_MODEL, Z_COLS - 1932), w_in.dtype)], axis=2)
    return {
        "g_pre_mix": row(p["g_pre_mix"]), "g_post_mix": row(p["g_post_mix"]),
        "g_pre_ffn": row(p["g_pre_ffn"]), "g_post_ffn": row(p["g_post_ffn"]),
        "w_in": w_in.astype(BF16), "w_out": p["w_out"].astype(BF16),
        "ffn_w_gu": p["ffn_w_gu"].astype(BF16), "ffn_w_down": p["ffn_w_down"].astype(BF16),
        "sg_ln_g": row(p["sg_ln_g"]),
        "sg_w": jnp.transpose(p["sg_w"], (0, 2, 1, 3)).reshape(n_layers, SG_CHUNK, N_HEADS * SG_CHUNK),
        "sg_bias": jnp.repeat(jnp.swapaxes(p["sg_b"], 1, 2), HEAD_DIM, axis=2),
        "cv_w": p["cv_w"], "cv_b": row(p["cv_b"]), "cv_ln_g": row(p["cv_ln_g"]), "cv_ln_b": row(p["cv_ln_b"]),
        "cv_pw": p["cv_pw"].astype(BF16), "cv_pw_b": row(p["cv_pw_b"]),
        "pool_w": _block_diag([p["pool_w"][:, gi] for gi in range(len(POOL_WINDOWS))]).astype(BF16),
        "pool_scale": row(p["pool_scale"]),
        "cmp_pe": jnp.concatenate([p["cmp_pos_k"], p["cmp_pos_v"]], axis=2),
        "cmp_w1": _interleave(p["cmp_w1_k"], p["cmp_w1_v"]).astype(BF16),
        "cmp_w2": _block_diag([p["cmp_w2_k"], p["cmp_w2_v"]]).astype(BF16),
    }


def _layer(layer, x2, batch, seq, prm):
    n = x2.shape[0]
    y_a, y_b, y_d, zq, zkv = _front(layer, x2.reshape(batch, seq, D_MODEL), prm)
    y_c = _nsa(zq, zkv, _compress(layer, zkv, prm))
    flat = lambda y: y.reshape(n, GW)
    return _back(layer, flat(y_a), flat(y_b), flat(y_c), flat(y_d), x2, prm)


_PARAM_NAMES = ("g_pre_mix", "g_post_mix", "g_pre_ffn", "g_post_ffn", "w_in", "sg_ln_g", "sg_w", "sg_b",
                "cv_w", "cv_b", "cv_ln_g", "cv_ln_b", "cv_pw", "cv_pw_b", "cmp_pos_k", "cmp_pos_v",
                "cmp_w1_k", "cmp_w2_k", "cmp_w1_v", "cmp_w2_v", "pool_w", "pool_scale", "w_out",
                "ffn_w_gu", "ffn_w_down")


def kernel(x, g_pre_mix, g_post_mix, g_pre_ffn, g_post_ffn, w_in, sg_ln_g, sg_w, sg_b, cv_w, cv_b, cv_ln_g, cv_ln_b, cv_pw, cv_pw_b, cmp_pos_k, cmp_pos_v, cmp_w1_k, cmp_w2_k, cmp_w1_v, cmp_w2_v, pool_w, pool_scale, w_out, ffn_w_gu, ffn_w_down):
    params = dict(zip(_PARAM_NAMES, (g_pre_mix, g_post_mix, g_pre_ffn, g_post_ffn, w_in, sg_ln_g, sg_w,
                                     sg_b, cv_w, cv_b, cv_ln_g, cv_ln_b, cv_pw, cv_pw_b, cmp_pos_k,
                                     cmp_pos_v, cmp_w1_k, cmp_w2_k, cmp_w1_v, cmp_w2_v, pool_w,
                                     pool_scale, w_out, ffn_w_gu, ffn_w_down)))
    batch, seq, _ = x.shape
    prm = _prepare(params)
    x2 = x.reshape(batch * seq, D_MODEL)
    for layer in range(g_pre_mix.shape[0]):
        x2 = _layer(layer, x2, batch, seq, prm)
    return x2.reshape(batch, seq, D_MODEL)
```

```python
import jax
import jax.numpy as jnp
from jax import lax
from jax.experimental import pallas as pl
from jax.experimental.pallas import tpu as pltpu

F32 = jnp.float32
BF16 = jnp.bfloat16

D_MODEL = 1024
GW = 256
HEAD_DIM = 64
N_HEADS = 4
SG_CHUNK = 128
CONV_WIDTH = 31
CMP_BLOCK = 32
CMP_STRIDE = 16
SLC_BLOCK = 64
SLC_TOPK = 8
WIN = 512
Q_BLOCK = 128
FORCE_BONUS = 1e4
NEG = -1e30
POOL_WINDOWS = (2, 4, 8, 16)
FFN_HIDDEN = 2816
RMS_EPS = 1e-6
LN_EPS = 1e-5
Z_COLS = 2048
Q_OFF = 1024
D_OFF = 1280
KV_OFF = 1536
KV_COLS = 512
N_CMP_PAD = 128
N_SLC = 32

TM = 512
TS = 1024
CONV_HALO = 32
POOL_HALO = 16
SUB = 64
SUBLANES = 8
POOL_PAD = SUBLANES
SHIFT_CHUNK = 128
LOG2E = 1.4426950408889634
SLC_TK = 256
SLC_ROW_GROUPS = 2
WINDOW_LATE_MAX_KEYS = 1024
SLC_SPLIT_KEYS = 768
AUG = 128
PAD_LANE = HEAD_DIM + N_SLC
MXU_DIM = 256
FFN_CHUNKS = (6 * MXU_DIM, 5 * MXU_DIM)
CMP_TB = 4
VMEM_LIMIT = 56 * 1024 * 1024


def _cparams(sem):
    return pltpu.CompilerParams(dimension_semantics=sem, vmem_limit_bytes=VMEM_LIMIT)


def _nt(a, b):
    return lax.dot_general(a, b, (((1,), (1,)), ((), ())), preferred_element_type=F32)


def _dot(a, b):
    return jnp.dot(a, b, preferred_element_type=F32)


def _rms(x, g):
    return (x * lax.rsqrt(jnp.mean(x * x, axis=-1, keepdims=True) + RMS_EPS)) * g


def _layernorm(x, g):
    mu = jnp.mean(x, axis=-1, keepdims=True)
    d = x - mu
    var = jnp.mean(d * d, axis=-1, keepdims=True)
    return (d * lax.rsqrt(var + LN_EPS)) * g


def _sgu_chunk(c, z_a, lng, w, bias, o_ref):
    lane_head = lax.broadcasted_iota(jnp.int32, (SG_CHUNK, GW), 1) // HEAD_DIM
    blk = z_a[c * SG_CHUNK:(c + 1) * SG_CHUNK, :]
    u = blk[:, :GW]
    vn = _layernorm(blk[:, GW:], lng).astype(BF16)
    zero = jnp.zeros_like(vn)
    v4 = jnp.concatenate([jnp.where(lane_head == h, vn, zero) for h in range(N_HEADS)], axis=0)
    sv = _dot(w, v4) + bias
    o_ref[0, c * SG_CHUNK:(c + 1) * SG_CHUNK, :] = (u * sv).astype(BF16)


def _conv_prep(si, z_b, hbuf):
    prev = hbuf[0, TS:TS + CONV_HALO, :]
    hbuf[0, 0:CONV_HALO, :] = jnp.where(si > 0, prev, 0.0)
    for r0 in range(0, TS, SUB):
        blk = z_b[r0:r0 + SUB, :]
        hbuf[0, CONV_HALO + r0:CONV_HALO + r0 + SUB, :] = blk[:, :GW] * jax.nn.sigmoid(blk[:, GW:])
    rows = CONV_HALO + TS - SUBLANES
    for r in range(1, SUBLANES):
        for j0 in range(0, rows, SHIFT_CHUNK):
            n = min(SHIFT_CHUNK, rows - j0)
            hbuf[r, j0:j0 + n, :] = hbuf[0, j0 + r:j0 + r + n, :]


def _conv_chunk(r0, cw, cb, lng, lnb, pw, pwb, o_ref, hbuf):
    lead = CONV_HALO - (CONV_WIDTH - 1)
    acc = jnp.zeros((SUB, GW), F32)
    for k in range(CONV_WIDTH):
        off = lead + k
        base = r0 + off - off % SUBLANES
        acc = acc + cw[k:k + 1, :] * hbuf[off % SUBLANES, base:base + SUB, :]
    y = _layernorm(acc + cb, lng) + lnb
    y = y * jax.nn.sigmoid(y)
    out = _dot(y.astype(BF16), pw) + pwb
    o_ref[0, r0:r0 + SUB, :] = out.astype(BF16)


def _pool_prep(si, z_d, xbuf, s2buf, s4buf, s8buf):
    data0 = POOL_PAD + POOL_HALO
    total = data0 + TS
    zeros = jnp.zeros((POOL_PAD, GW), F32)
    prev = xbuf[total - POOL_HALO:total, :]
    xbuf[0:POOL_PAD, :] = zeros
    s2buf[0:POOL_PAD, :] = zeros
    s4buf[0:POOL_PAD, :] = zeros
    xbuf[POOL_PAD:data0, :] = jnp.where(si > 0, prev, 0.0)
    xbuf[data0:total, :] = z_d
    for src, dst, shift in ((xbuf, s2buf, 1), (s2buf, s4buf, 2), (s4buf, s8buf, 4)):
        for j0 in range(POOL_PAD, total, SHIFT_CHUNK):
            n = min(SHIFT_CHUNK, total - j0)
            dst[j0:j0 + n, :] = src[j0:j0 + n, :] + src[j0 - shift:j0 - shift + n, :]


def _pool_chunk(si, r0, w, scale, o_ref, xbuf, s2buf, s4buf, s8buf):
    lane_grp = lax.broadcasted_iota(jnp.int32, (1, GW), 1) // (GW // len(POOL_WINDOWS))
    win = jnp.zeros((1, GW), jnp.int32)
    for gi, wlen in enumerate(POOL_WINDOWS):
        win = jnp.where(lane_grp == gi, wlen, win)
    j = POOL_PAD + POOL_HALO + r0
    x = xbuf[j:j + SUB, :]
    s8 = s8buf[j:j + SUB, :]
    s16 = s8 + s8buf[j - 8:j - 8 + SUB, :]
    acc = jnp.where(lane_grp == 0, s2buf[j:j + SUB, :],
                    jnp.where(lane_grp == 1, s4buf[j:j + SUB, :],
                              jnp.where(lane_grp == 2, s8, s16)))
    t1 = si * TS + r0 + 1 + lax.broadcasted_iota(jnp.int32, (SUB, GW), 0)
    cnt = jnp.minimum(t1, win).astype(F32)
    diff = acc / cnt - x
    out = _dot(diff.astype(BF16), w) * scale
    o_ref[0, r0:r0 + SUB, :] = out.astype(BF16)


def _front_kernel(x_ref, g_ref, w_ref,
                  sg_lng_ref, sg_w_ref, sg_bias_ref,
                  cw_ref, cb_ref, cv_lng_ref, cv_lnb_ref, pw_ref, pwb_ref,
                  pool_w_ref, pool_scale_ref,
                  ya_ref, yb_ref, yd_ref, zq_ref, zkv_ref,
                  hbuf, xbuf, s2buf, s4buf, s8buf):
    si = pl.program_id(1)

    @pl.when(si == 0)
    def _():
        hbuf[0, TS:TS + CONV_HALO, :] = jnp.zeros((CONV_HALO, GW), F32)
        xbuf[POOL_PAD + TS:POOL_PAD + POOL_HALO + TS, :] = jnp.zeros((POOL_HALO, GW), F32)

    conv_args = (cw_ref[...], cb_ref[...], cv_lng_ref[...], cv_lnb_ref[...], pw_ref[...], pwb_ref[...],
                 yb_ref, hbuf)
    pool_bufs = (xbuf, s2buf, s4buf, s8buf)
    conv = lambda k: _conv_chunk(k * SUB, *conv_args)
    pool = lambda k: _pool_chunk(si, k * SUB, pool_w_ref[...], pool_scale_ref[...], yd_ref, *pool_bufs)

    h = _rms(x_ref[0], g_ref[...]).astype(BF16)
    quarter = TS // SUB // 4
    _conv_prep(si, _dot(h, w_ref[:, 2 * GW:4 * GW]), hbuf)
    z_qd = _dot(h, w_ref[:, Q_OFF:Q_OFF + 2 * GW])
    for k in range(0, quarter):
        conv(k)
    zq_ref[0] = z_qd[:, :GW]
    _pool_prep(si, z_qd[:, GW:], *pool_bufs)
    z_a = _dot(h, w_ref[:, 0:2 * GW])
    for k in range(quarter, 2 * quarter):
        conv(k)
    for k in range(0, 2 * quarter):
        pool(k)
    zkv_ref[0] = _dot(h, w_ref[:, KV_OFF:KV_OFF + KV_COLS])
    for q in (2, 3):
        for k in range(q * quarter, (q + 1) * quarter):
            conv(k)
        for k in range(q * quarter, (q + 1) * quarter):
            pool(k)
    row = lax.broadcasted_iota(jnp.int32, (SG_CHUNK, 4 * SG_CHUNK), 0)
    col = lax.broadcasted_iota(jnp.int32, (SG_CHUNK, 4 * SG_CHUNK), 1) % SG_CHUNK
    sg_w = jnp.where(row >= col, sg_w_ref[...], 0.0).astype(BF16)
    for c in range(TS // SG_CHUNK):
        _sgu_chunk(c, z_a, sg_lng_ref[...], sg_w, sg_bias_ref[...], ya_ref)


def _layer_spec(arr, layer, **kw):
    tail = arr.shape[1:]
    return pl.BlockSpec((None,) + tail, lambda *_: (layer,) + (0,) * len(tail), **kw)


_FRONT_PARAMS = ("g_pre_mix", "w_in", "sg_ln_g", "sg_w", "sg_bias", "cv_w", "cv_b", "cv_ln_g", "cv_ln_b",
                 "cv_pw", "cv_pw_b", "pool_w", "pool_scale")


def _front(layer, x3, prm):
    b, s, _ = x3.shape
    tile = lambda width: pl.BlockSpec((1, TS, width), lambda bi, si: (bi, si, 0))
    pool_rows = POOL_PAD + POOL_HALO + TS
    weights = [prm[k] for k in _FRONT_PARAMS]
    return pl.pallas_call(
        _front_kernel,
        grid=(b, s // TS),
        in_specs=[tile(D_MODEL)] + [_layer_spec(w, layer) for w in weights],
        out_specs=[tile(GW), tile(GW), tile(GW), tile(GW), tile(KV_COLS)],
        out_shape=[
            jax.ShapeDtypeStruct((b, s, GW), BF16),
            jax.ShapeDtypeStruct((b, s, GW), BF16),
            jax.ShapeDtypeStruct((b, s, GW), BF16),
            jax.ShapeDtypeStruct((b, s, GW), F32),
            jax.ShapeDtypeStruct((b, s, KV_COLS), F32),
        ],
        scratch_shapes=[pltpu.VMEM((SUBLANES, CONV_HALO + TS, GW), F32)]
        + [pltpu.VMEM((pool_rows, GW), F32)] * 4,
        compiler_params=_cparams(("parallel", "arbitrary")),
        name="front",
    )(x3, *weights)


def _compress_kernel(x_ref, pe_ref, w1_ref, w2_ref, o_ref):
    m = CMP_TB * N_CMP_PAD
    p0 = jnp.zeros((m, 128), F32)
    p1 = jnp.zeros((m, 128), F32)
    for r in range(CMP_STRIDE):
        xr = jnp.concatenate([x_ref[bi, pl.ds(r, N_CMP_PAD, stride=CMP_STRIDE), :] for bi in range(CMP_TB)],
                             axis=0)
        p0 = p0 + _dot((xr + pe_ref[r:r + 1, :]).astype(BF16), w1_ref[r])
        p1 = p1 + _dot((xr + pe_ref[CMP_STRIDE + r:CMP_STRIDE + r + 1, :]).astype(BF16),
                       w1_ref[CMP_STRIDE + r])
    pre = p0 + pltpu.roll(p1, m - 1, axis=0)
    hid = pre * jax.nn.sigmoid(pre)
    out = _dot(hid.astype(BF16), w2_ref[...])
    rowid = lax.broadcasted_iota(jnp.int32, (m, 128), 0) % N_CMP_PAD
    out = jnp.where(rowid < N_CMP_PAD - 1, out, 0.0)
    o_ref[...] = out.reshape(CMP_TB, N_CMP_PAD, 128)


def _compress(layer, zkv, prm):
    b, s, _ = zkv.shape
    weights = [prm[k] for k in ("cmp_pe", "cmp_w1", "cmp_w2")]
    return pl.pallas_call(
        _compress_kernel,
        grid=(b // CMP_TB,),
        in_specs=[pl.BlockSpec((CMP_TB, s, 128), lambda i: (i, 0, 0))] + [_layer_spec(w, layer) for w in weights],
        out_specs=pl.BlockSpec((CMP_TB, N_CMP_PAD, 128), lambda i: (i, 0, 0)),
        out_shape=jax.ShapeDtypeStruct((b, N_CMP_PAD, 128), F32),
        compiler_params=_cparams(("parallel",)),
        name="nsa_compress",
    )(zkv, *weights)


def _nsa_kernel(q_ref, kv_ref, kcv_ref, o_ref, ks_aug, kts, vs_aug, kw_aug, vw_aug, gsel):
    i = pl.program_id(1)
    seq = kv_ref.shape[1]
    nq = Q_BLOCK

    @pl.when(i == 0)
    def _():
        eye = jnp.where(lax.broadcasted_iota(jnp.int32, (AUG, AUG), 0)
                        == lax.broadcasted_iota(jnp.int32, (AUG, AUG), 1), 1.0, 0.0).astype(BF16)
        lane_p = lax.broadcasted_iota(jnp.int32, (WIN, AUG), 1)
        kw_aug[0:WIN, :] = jnp.where(lane_p == PAD_LANE, 1.0, 0.0).astype(BF16)
        vw_aug[0:WIN, :] = jnp.zeros((WIN, 2 * AUG), BF16)
        gsel[...] = jnp.where(lax.broadcasted_iota(jnp.int32, gsel.shape, 0)
                              == lax.broadcasted_iota(jnp.int32, gsel.shape, 1) // AUG, 1.0, 0.0).astype(BF16)
        lane = lax.broadcasted_iota(jnp.int32, (256, AUG), 1)
        ones = jnp.ones((256, AUG), BF16)
        for r in range(0, seq, 256):
            blk = kv_ref[0, r:r + 256, :]
            t_s = blk[:, 128:256]
            t_w = blk[:, 256:384]
            key_blk = (r + lax.broadcasted_iota(jnp.int32, (256, AUG), 0)) // SLC_BLOCK
            onehot = jnp.where(lane - HEAD_DIM == key_blk, 1.0, 0.0)
            ks_aug[r:r + 256, :] = jnp.where(lane < HEAD_DIM, t_s, onehot).astype(BF16)
            v_s = jnp.where(lane < HEAD_DIM, pltpu.roll(t_s, HEAD_DIM, axis=1), 0.0).astype(BF16)
            vs_aug[r:r + 256, :] = jnp.concatenate([v_s, ones], axis=1)
            kw_aug[WIN + r:WIN + r + 256, :] = jnp.where(lane < HEAD_DIM, t_w, 0.0).astype(BF16)
            v_w = jnp.where(lane < HEAD_DIM, pltpu.roll(t_w, HEAD_DIM, axis=1), 0.0).astype(BF16)
            vw_aug[WIN + r:WIN + r + 256, :] = jnp.concatenate([v_w, ones], axis=1)
        for c in range(0, seq, SLC_TK):
            kts[:, c:c + SLC_TK] = _nt(eye, ks_aug[c:c + SLC_TK, :]).astype(BF16)

    per = SLC_TK // nq
    n_dyn = (i + per - 1) // per
    for n_kt in range(seq // SLC_TK + 1):
        @pl.when(n_dyn == n_kt)
        def _(n_kt=n_kt):
            _nsa_tile(n_kt, i, q_ref, kv_ref, kcv_ref, o_ref, ks_aug, kts, vs_aug, kw_aug, vw_aug, gsel)


def _nsa_tile(n_kt, i, q_ref, kv_ref, kcv_ref, o_ref, ks_aug, kts, vs_aug, kw_aug, vw_aug, gsel):
    nq = Q_BLOCK
    hq = N_HEADS * nq
    eye = jnp.where(lax.broadcasted_iota(jnp.int32, (AUG, AUG), 0)
                    == lax.broadcasted_iota(jnp.int32, (AUG, AUG), 1), 1.0, 0.0).astype(BF16)
    q0 = pl.multiple_of(i * nq, nq)
    q = q_ref[0] * (HEAD_DIM ** -0.5 * LOG2E)
    lane_q = lax.broadcasted_iota(jnp.int32, (nq, AUG), 1)

    def stack_heads(extra):
        tiles = []
        for h in range(N_HEADS):
            t = q[:, AUG * (h // 2):AUG * (h // 2 + 1)]
            if h % 2:
                t = pltpu.roll(t, HEAD_DIM, axis=1)
            tiles.append(jnp.where(lane_q < HEAD_DIM, t, extra))
        return jnp.concatenate(tiles, axis=0).astype(BF16)

    pad_bias = jnp.where(lane_q == PAD_LANE, NEG, 0.0)
    qw = stack_heads(pad_bias)
    trow = q0 + lax.broadcasted_iota(jnp.int32, (nq, 1), 0)
    row_l = lax.broadcasted_iota(jnp.int32, (nq, nq), 0)
    col_l = lax.broadcasted_iota(jnp.int32, (nq, nq), 1)
    tri_le = (col_l <= row_l)[None]
    tri_gt = (col_l > row_l)[None]

    def window_branch():
        nk = WIN + nq
        sw = _nt(qw, kw_aug[pl.ds(q0, nk), :]).reshape(N_HEADS, nq, nk)
        sw = jnp.concatenate([jnp.where(tri_gt, sw[:, :, :nq], NEG), sw[:, :, nq:WIN],
                              jnp.where(tri_le, sw[:, :, WIN:], NEG)], axis=-1)
        pw = jnp.exp2(sw - jnp.max(sw, axis=-1, keepdims=True)).astype(BF16)
        return _dot(pw.reshape(hq, nk), vw_aug[pl.ds(q0, nk), :])

    window_first = not (1 <= n_kt * SLC_TK <= WINDOW_LATE_MAX_KEYS)
    if window_first:
        o_win = window_branch()

    kcv = kcv_ref[0]
    lane_c = lax.broadcasted_iota(jnp.int32, (N_CMP_PAD, AUG), 1)
    kc = jnp.where(lane_c < HEAD_DIM, kcv, 0.0).astype(BF16)
    vc = jnp.where(lane_c < HEAD_DIM, pltpu.roll(kcv, HEAD_DIM, axis=1), 0.0).astype(BF16)
    sd = _nt(qw, ks_aug[pl.ds(q0, nq), :]).reshape(N_HEADS, nq, nq)
    sd = jnp.where(tri_le, sd, NEG).reshape(hq, nq)
    md = jnp.max(sd, axis=-1, keepdims=True)

    s = _nt(qw, kc).reshape(N_HEADS, nq, N_CMP_PAD)
    cidx = lax.broadcasted_iota(jnp.int32, (nq, N_CMP_PAD), 1)
    cmask = (cidx * CMP_STRIDE + (CMP_BLOCK - 1) <= trow) & (cidx < N_CMP_PAD - 1)
    sm = jnp.where(cmask[None], s, NEG)
    e = jnp.exp2(sm - jnp.max(sm, axis=-1, keepdims=True))
    p = e / jnp.sum(e, axis=-1, keepdims=True)
    p = jnp.where((trow >= CMP_BLOCK - 1)[None], p, 0.0)
    o_cmp = _dot(p.reshape(hq, N_CMP_PAD).astype(BF16), vc)

    if not window_first:
        o_win = window_branch()

    psum = jnp.sum(p, axis=0)
    p_hi = psum.astype(BF16)
    p_lo = (psum - p_hi.astype(F32)).astype(BF16)
    jrow = lax.broadcasted_iota(jnp.int32, (N_SLC, N_CMP_PAD), 0)
    ccol = lax.broadcasted_iota(jnp.int32, (N_SLC, N_CMP_PAD), 1)
    ov = ((ccol * CMP_STRIDE <= jrow * SLC_BLOCK + (SLC_BLOCK - 1))
          & (ccol * CMP_STRIDE + (CMP_BLOCK - 1) >= jrow * SLC_BLOCK)
          & (ccol < N_CMP_PAD - 1))
    ov = jnp.where(ov, 1.0, 0.0).astype(BF16)
    imp = _nt(ov, p_hi) + _nt(ov, p_lo)
    jj = lax.broadcasted_iota(jnp.int32, (N_SLC, nq), 0)
    tt = q0 + lax.broadcasted_iota(jnp.int32, (N_SLC, nq), 1)
    cur = tt // SLC_BLOCK
    valid = jj * SLC_BLOCK <= tt
    forced = (jj == 0) | (jj == cur) | (jj == cur - 1)
    score = jnp.where(valid, imp + jnp.where(forced, FORCE_BONUS, 0.0), NEG)
    rank = jnp.zeros((N_SLC, nq), F32)
    for j2 in range(N_SLC):
        other = score[j2:j2 + 1, :]
        beats = (other > score) | ((other == score) & (jj > j2))
        rank = rank + jnp.where(beats, 1.0, 0.0)
    sel_t = jnp.where((rank < SLC_TOPK) & (score > NEG / 2), 1.0, 0.0)
    bias_t = jnp.where((sel_t > 0.5) & (jj < 2 * i), 0.0, NEG)
    bias_t = jnp.concatenate([jnp.zeros((HEAD_DIM, nq), F32), bias_t,
                              jnp.zeros((AUG - HEAD_DIM - N_SLC, nq), F32)], axis=0).astype(BF16)
    blk_bias = _nt(eye, bias_t)

    if n_kt:
        n_main = n_kt * SLC_TK
        qs = stack_heads(blk_bias + pad_bias)
        k_main = kts[:, 0:n_main]
        v_main = vs_aug[0:n_main, :]
        v_diag = vs_aug[pl.ds(q0, nq), :]
        n_grp = SLC_ROW_GROUPS if n_main >= SLC_SPLIT_KEYS else 1
        rows = hq // n_grp
        by_rows = lambda a, b: jnp.concatenate([_dot(a[r0:r0 + rows], b) for r0 in range(0, hq, rows)], axis=0)
        s_main = by_rows(qs, k_main)
        m = jnp.maximum(md, jnp.max(s_main, axis=-1, keepdims=True))
        p_main = jnp.exp2(s_main - m).astype(BF16)
        p_diag = jnp.exp2(sd - m).astype(BF16)
        o_slc = _dot(p_diag, v_diag) + by_rows(p_main, v_main)
    else:
        o_slc = _dot(jnp.exp2(sd - md).astype(BF16), vs_aug[pl.ds(q0, nq), :])

    g = jax.nn.sigmoid(kv_ref[0, pl.ds(q0, nq), 384:384 + AUG])
    g_hi = g.astype(BF16)
    g_lo = (g - g_hi.astype(F32)).astype(BF16)
    g_rep = _dot(g_hi, gsel[...]) + _dot(g_lo, gsel[...])
    gate = lambda br: jnp.concatenate(
        [g_rep[:, (3 * h + br) * AUG:(3 * h + br + 1) * AUG] for h in range(N_HEADS)], axis=0)
    o = (gate(0) * o_cmp
         + (gate(1) / o_slc[:, AUG:]) * o_slc[:, :AUG]
         + (gate(2) / o_win[:, AUG:]) * o_win[:, :AUG])
    halves = [jnp.where(lane_q < HEAD_DIM, o[2 * a * nq:(2 * a + 1) * nq, :],
                        pltpu.roll(o[(2 * a + 1) * nq:(2 * a + 2) * nq, :], HEAD_DIM, axis=1))
              for a in range(N_HEADS // 2)]
    o_ref[0] = jnp.concatenate(halves, axis=1).astype(BF16)


def _nsa(zq, zkv, kcv):
    b, s, _ = zq.shape
    return pl.pallas_call(
        _nsa_kernel,
        grid=(b, s // Q_BLOCK),
        in_specs=[
            pl.BlockSpec((1, Q_BLOCK, GW), lambda bi, qi: (bi, qi, 0)),
            pl.BlockSpec((1, s, KV_COLS), lambda bi, qi: (bi, 0, 0)),
            pl.BlockSpec((1, N_CMP_PAD, 128), lambda bi, qi: (bi, 0, 0)),
        ],
        out_specs=pl.BlockSpec((1, Q_BLOCK, GW), lambda bi, qi: (bi, qi, 0)),
        out_shape=jax.ShapeDtypeStruct((b, s, GW), BF16),
        scratch_shapes=[
            pltpu.VMEM((s, AUG), BF16),
            pltpu.VMEM((AUG, s), BF16),
            pltpu.VMEM((s, 2 * AUG), BF16),
            pltpu.VMEM((WIN + s, AUG), BF16),
            pltpu.VMEM((WIN + s, 2 * AUG), BF16),
            pltpu.VMEM((AUG, 3 * N_HEADS * AUG), BF16),
        ],
        compiler_params=_cparams(("parallel", "arbitrary")),
        name="nsa_attn",
    )(zq, zkv, kcv)


def _back_kernel(ya_ref, yb_ref, yc_ref, yd_ref, x_ref, wo_ref, gmix_ref, gpre_ref, wgu_ref, wd_ref,
                 gpost_ref, o_ref):
    xs, hs = [], []
    for r0 in range(0, TM, TM // 2):
        rows = slice(r0, r0 + TM // 2)
        y = jnp.concatenate([ya_ref[rows, :], yb_ref[rows, :], yc_ref[rows, :], yd_ref[rows, :]], axis=1)
        x_half = x_ref[rows, :] + _rms(_dot(y, wo_ref[...]), gmix_ref[...])
        xs.append(x_half)
        hs.append(_rms(x_half, gpre_ref[...]).astype(BF16))
    x = jnp.concatenate(xs, axis=0)
    h = jnp.concatenate(hs, axis=0)
    f = jnp.zeros((TM, D_MODEL), F32)
    c0 = 0
    for width in FFN_CHUNKS:
        gate = _dot(h, wgu_ref[:, c0:c0 + width])
        up = _dot(h, wgu_ref[:, FFN_HIDDEN + c0:FFN_HIDDEN + c0 + width])
        act = ((gate * jax.nn.sigmoid(gate)) * up).astype(BF16)
        f = f + _dot(act, wd_ref[c0:c0 + width, :])
        c0 += width
    o_ref[...] = x + _rms(f, gpost_ref[...])


_BACK_PARAMS = ("w_out", "g_post_mix", "g_pre_ffn", "ffn_w_gu", "ffn_w_down", "g_post_ffn")


def _back(layer, ya, yb, yc, yd, x2, prm):
    n = x2.shape[0]
    yspec = pl.BlockSpec((TM, GW), lambda i: (i, 0))
    weights = [prm[k] for k in _BACK_PARAMS]
    return pl.pallas_call(
        _back_kernel,
        grid=(n // TM,),
        in_specs=[yspec, yspec, yspec, yspec, pl.BlockSpec((TM, D_MODEL), lambda i: (i, 0))]
        + [_layer_spec(w, layer, pipeline_mode=pl.Buffered(1)) for w in weights],
        out_specs=pl.BlockSpec((TM, D_MODEL), lambda i: (i, 0)),
        out_shape=jax.ShapeDtypeStruct((n, D_MODEL), F32),
        compiler_params=_cparams(("parallel",)),
        name="back",
    )(ya, yb, yc, yd, x2, *weights)


def _interleave(wk, wv):
    n_layers = wk.shape[0]
    z = jnp.zeros((n_layers, CMP_BLOCK, HEAD_DIM, HEAD_DIM), wk.dtype)
    wk4 = wk.reshape(n_layers, CMP_BLOCK, HEAD_DIM, HEAD_DIM)
    wv4 = wv.reshape(n_layers, CMP_BLOCK, HEAD_DIM, HEAD_DIM)
    return jnp.concatenate([jnp.concatenate([wk4, z], axis=3), jnp.concatenate([z, wv4], axis=3)], axis=2)


def _block_diag(mats):
    n = len(mats)
    rows = []
    for a, m in enumerate(mats):
        rows.append(jnp.concatenate([m if a == c else jnp.zeros_like(m) for c in range(n)], axis=2))
    return jnp.concatenate(rows, axis=1)


def _prepare(p):
    n_layers = p["w_in"].shape[0]
    row = lambda v: v.reshape(n_layers, 1, -1)
    w_in = p["w_in"]
    w_in = jnp.concatenate(
        [w_in[:, :, :1280], w_in[:, :, 1676:1932], w_in[:, :, 1280:1676],
         jnp.zeros((n_layers, D_MODEL, Z_COLS - 1932), w_in.dtype)], axis=2)
    return {
        "g_pre_mix": row(p["g_pre_mix"]), "g_post_mix": row(p["g_post_mix"]),
        "g_pre_ffn": row(p["g_pre_ffn"]), "g_post_ffn": row(p["g_post_ffn"]),
        "w_in": w_in.astype(BF16), "w_out": p["w_out"].astype(BF16),
        "ffn_w_gu": p["ffn_w_gu"].astype(BF16), "ffn_w_down": p["ffn_w_down"].astype(BF16),
        "sg_ln_g": row(p["sg_ln_g"]),
        "sg_w": jnp.transpose(p["sg_w"], (0, 2, 1, 3)).reshape(n_layers, SG_CHUNK, N_HEADS * SG_CHUNK),
        "sg_bias": jnp.repeat(jnp.swapaxes(p["sg_b"], 1, 2), HEAD_DIM, axis=2),
        "cv_w": p["cv_w"], "cv_b": row(p["cv_b"]), "cv_ln_g": row(p["cv_ln_g"]), "cv_ln_b": row(p["cv_ln_b"]),
        "cv_pw": p["cv_pw"].astype(BF16), "cv_pw_b": row(p["cv_pw_b"]),
        "pool_w": _block_diag([p["pool_w"][:, gi] for gi in range(len(POOL_WINDOWS))]).astype(BF16),
        "pool_scale": row(p["pool_scale"]),
        "cmp_pe": jnp.concatenate([p["cmp_pos_k"], p["cmp_pos_v"]], axis=2),
        "cmp_w1": _interleave(p["cmp_w1_k"], p["cmp_w1_v"]).astype(BF16),
        "cmp_w2": _block_diag([p["cmp_w2_k"], p["cmp_w2_v"]]).astype(BF16),
    }


def _layer(layer, x2, batch, seq, prm):
    n = x2.shape[0]
    y_a, y_b, y_d, zq, zkv = _front(layer, x2.reshape(batch, seq, D_MODEL), prm)
    y_c = _nsa(zq, zkv, _compress(layer, zkv, prm))
    flat = lambda y: y.reshape(n, GW)
    return _back(layer, flat(y_a), flat(y_b), flat(y_c), flat(y_d), x2, prm)


_PARAM_NAMES = ("g_pre_mix", "g_post_mix", "g_pre_ffn", "g_post_ffn", "w_in", "sg_ln_g", "sg_w", "sg_b",
                "cv_w", "cv_b", "cv_ln_g", "cv_ln_b", "cv_pw", "cv_pw_b", "cmp_pos_k", "cmp_pos_v",
                "cmp_w1_k", "cmp_w2_k", "cmp_w1_v", "cmp_w2_v", "pool_w", "pool_scale", "w_out",
                "ffn_w_gu", "ffn_w_down")


def kernel(x, g_pre_mix, g_post_mix, g_pre_ffn, g_post_ffn, w_in, sg_ln_g, sg_w, sg_b, cv_w, cv_b, cv_ln_g, cv_ln_b, cv_pw, cv_pw_b, cmp_pos_k, cmp_pos_v, cmp_w1_k, cmp_w2_k, cmp_w1_v, cmp_w2_v, pool_w, pool_scale, w_out, ffn_w_gu, ffn_w_down):
    params = dict(zip(_PARAM_NAMES, (g_pre_mix, g_post_mix, g_pre_ffn, g_post_ffn, w_in, sg_ln_g, sg_w,
                                     sg_b, cv_w, cv_b, cv_ln_g, cv_ln_b, cv_pw, cv_pw_b, cmp_pos_k,
                                     cmp_pos_v, cmp_w1_k, cmp_w2_k, cmp_w1_v, cmp_w2_v, pool_w,
                                     pool_scale, w_out, ffn_w_gu, ffn_w_down)))
    batch, seq, _ = x.shape
    prm = _prepare(params)
    x2 = x.reshape(batch * seq, D_MODEL)
    for layer in range(g_pre_mix.shape[0]):
        x2 = _layer(layer, x2, batch, seq, prm)
    return x2.reshape(batch, seq, D_MODEL)
```

```python
import jax
import jax.numpy as jnp
from jax import lax
from jax.experimental import pallas as pl
from jax.experimental.pallas import tpu as pltpu

F32 = jnp.float32
BF16 = jnp.bfloat16

D_MODEL = 1024
GW = 256
HEAD_DIM = 64
N_HEADS = 4
SG_CHUNK = 128
CONV_WIDTH = 31
CMP_BLOCK = 32
CMP_STRIDE = 16
SLC_BLOCK = 64
SLC_TOPK = 8
WIN = 512
Q_BLOCK = 128
FORCE_BONUS = 1e4
NEG = -1e30
POOL_WINDOWS = (2, 4, 8, 16)
FFN_HIDDEN = 2816
RMS_EPS = 1e-6
LN_EPS = 1e-5
Z_COLS = 2048
Q_OFF = 1024
D_OFF = 1280
KV_OFF = 1536
KV_COLS = 512
N_CMP_PAD = 128
N_SLC = 32

TM = 512
TS = 1024
CONV_HALO = 32
POOL_HALO = 16
SUB = 64
NORM_CHUNK = 128
SUBLANES = 8
POOL_PAD = SUBLANES
SHIFT_CHUNK = 128
LOG2E = 1.4426950408889634
SLC_TK = 256
SLC_ROW_GROUPS = 2
WINDOW_LATE_MAX_KEYS = 1024
SLC_SPLIT_KEYS = 768
AUG = 128
PAD_LANE = HEAD_DIM + N_SLC
MXU_DIM = 256
FFN_CHUNKS = (6 * MXU_DIM, 5 * MXU_DIM)
CMP_TB = 4
VMEM_LIMIT = 56 * 1024 * 1024


def _cparams(sem):
    return pltpu.CompilerParams(dimension_semantics=sem, vmem_limit_bytes=VMEM_LIMIT)


def _nt(a, b):
    return lax.dot_general(a, b, (((1,), (1,)), ((), ())), preferred_element_type=F32)


def _dot(a, b):
    return jnp.dot(a, b, preferred_element_type=F32)


def _rms(x, g):
    return (x * lax.rsqrt(jnp.mean(x * x, axis=-1, keepdims=True) + RMS_EPS)) * g


def _layernorm(x, g):
    mu = jnp.mean(x, axis=-1, keepdims=True)
    d = x - mu
    var = jnp.mean(d * d, axis=-1, keepdims=True)
    return (d * lax.rsqrt(var + LN_EPS)) * g


def _sgu_chunk(c, z_a, lng, w, bias, o_ref):
    lane_head = lax.broadcasted_iota(jnp.int32, (SG_CHUNK, GW), 1) // HEAD_DIM
    blk = z_a[c * SG_CHUNK:(c + 1) * SG_CHUNK, :]
    u = blk[:, :GW]
    vn = _layernorm(blk[:, GW:], lng).astype(BF16)
    zero = jnp.zeros_like(vn)
    v4 = jnp.concatenate([jnp.where(lane_head == h, vn, zero) for h in range(N_HEADS)], axis=0)
    sv = _dot(w, v4) + bias
    o_ref[0, c * SG_CHUNK:(c + 1) * SG_CHUNK, :] = (u * sv).astype(BF16)


def _conv_prep(si, z_b, hbuf):
    prev = hbuf[0, TS:TS + CONV_HALO, :]
    hbuf[0, 0:CONV_HALO, :] = jnp.where(si > 0, prev, 0.0)
    for r0 in range(0, TS, SUB):
        blk = z_b[r0:r0 + SUB, :]
        hbuf[0, CONV_HALO + r0:CONV_HALO + r0 + SUB, :] = blk[:, :GW] * jax.nn.sigmoid(blk[:, GW:])
    rows = CONV_HALO + TS - SUBLANES
    for r in range(1, SUBLANES):
        for j0 in range(0, rows, SHIFT_CHUNK):
            n = min(SHIFT_CHUNK, rows - j0)
            hbuf[r, j0:j0 + n, :] = hbuf[0, j0 + r:j0 + r + n, :]


def _conv_chunk(r0, cw, cb, lng, lnb, pw, pwb, o_ref, hbuf):
    lead = CONV_HALO - (CONV_WIDTH - 1)
    acc = jnp.zeros((SUB, GW), F32)
    for k in range(CONV_WIDTH):
        off = lead + k
        base = r0 + off - off % SUBLANES
        acc = acc + cw[k:k + 1, :] * hbuf[off % SUBLANES, base:base + SUB, :]
    y = _layernorm(acc + cb, lng) + lnb
    y = y * jax.nn.sigmoid(y)
    out = _dot(y.astype(BF16), pw) + pwb
    o_ref[0, r0:r0 + SUB, :] = out.astype(BF16)


def _pool_prep(si, z_d, xbuf, s2buf, s4buf, s8buf):
    data0 = POOL_PAD + POOL_HALO
    total = data0 + TS
    zeros = jnp.zeros((POOL_PAD, GW), F32)
    prev = xbuf[total - POOL_HALO:total, :]
    xbuf[0:POOL_PAD, :] = zeros
    s2buf[0:POOL_PAD, :] = zeros
    s4buf[0:POOL_PAD, :] = zeros
    xbuf[POOL_PAD:data0, :] = jnp.where(si > 0, prev, 0.0)
    xbuf[data0:total, :] = z_d
    for src, dst, shift in ((xbuf, s2buf, 1), (s2buf, s4buf, 2), (s4buf, s8buf, 4)):
        for j0 in range(POOL_PAD, total, SHIFT_CHUNK):
            n = min(SHIFT_CHUNK, total - j0)
            dst[j0:j0 + n, :] = src[j0:j0 + n, :] + src[j0 - shift:j0 - shift + n, :]


def _pool_chunk(si, r0, w, scale, o_ref, xbuf, s2buf, s4buf, s8buf):
    lane_grp = lax.broadcasted_iota(jnp.int32, (1, GW), 1) // (GW // len(POOL_WINDOWS))
    win = jnp.zeros((1, GW), jnp.int32)
    for gi, wlen in enumerate(POOL_WINDOWS):
        win = jnp.where(lane_grp == gi, wlen, win)
    j = POOL_PAD + POOL_HALO + r0
    x = xbuf[j:j + SUB, :]
    s8 = s8buf[j:j + SUB, :]
    s16 = s8 + s8buf[j - 8:j - 8 + SUB, :]
    acc = jnp.where(lane_grp == 0, s2buf[j:j + SUB, :],
                    jnp.where(lane_grp == 1, s4buf[j:j + SUB, :],
                              jnp.where(lane_grp == 2, s8, s16)))
    t1 = si * TS + r0 + 1 + lax.broadcasted_iota(jnp.int32, (SUB, GW), 0)
    cnt = jnp.minimum(t1, win).astype(F32)
    diff = acc / cnt - x
    out = _dot(diff.astype(BF16), w) * scale
    o_ref[0, r0:r0 + SUB, :] = out.astype(BF16)


def _front_kernel(x_ref, g_ref, w_ref,
                  sg_lng_ref, sg_w_ref, sg_bias_ref,
                  cw_ref, cb_ref, cv_lng_ref, cv_lnb_ref, pw_ref, pwb_ref,
                  pool_w_ref, pool_scale_ref,
                  ya_ref, yb_ref, yd_ref, zq_ref, zkv_ref,
                  hbuf, xbuf, s2buf, s4buf, s8buf, h_sc, za_sc):
    si = pl.program_id(1)

    @pl.when(si == 0)
    def _():
        hbuf[0, TS:TS + CONV_HALO, :] = jnp.zeros((CONV_HALO, GW), F32)
        xbuf[POOL_PAD + TS:POOL_PAD + POOL_HALO + TS, :] = jnp.zeros((POOL_HALO, GW), F32)

    conv_args = (cw_ref[...], cb_ref[...], cv_lng_ref[...], cv_lnb_ref[...], pw_ref[...], pwb_ref[...],
                 yb_ref, hbuf)
    pool_bufs = (xbuf, s2buf, s4buf, s8buf)
    conv = lambda k: _conv_chunk(k * SUB, *conv_args)
    pool = lambda k: _pool_chunk(si, k * SUB, pool_w_ref[...], pool_scale_ref[...], yd_ref, *pool_bufs)

    g_in = g_ref[...]
    for r0 in range(0, TS, NORM_CHUNK):
        h_sc[r0:r0 + NORM_CHUNK, :] = _rms(x_ref[0, r0:r0 + NORM_CHUNK, :], g_in).astype(BF16)
    quarter = TS // SUB // 4
    _conv_prep(si, _dot(h_sc[...], w_ref[:, 2 * GW:4 * GW]), hbuf)
    z_qd = _dot(h_sc[...], w_ref[:, Q_OFF:Q_OFF + 2 * GW])
    for k in range(0, quarter):
        conv(k)
    zq_ref[0] = z_qd[:, :GW]
    _pool_prep(si, z_qd[:, GW:], *pool_bufs)
    za_sc[...] = _dot(h_sc[...], w_ref[:, 0:2 * GW])
    for k in range(quarter, 2 * quarter):
        conv(k)
    for k in range(0, 2 * quarter):
        pool(k)
    zkv_ref[0] = _dot(h_sc[...], w_ref[:, KV_OFF:KV_OFF + KV_COLS])
    for q in (2, 3):
        for k in range(q * quarter, (q + 1) * quarter):
            conv(k)
        for k in range(q * quarter, (q + 1) * quarter):
            pool(k)
    row = lax.broadcasted_iota(jnp.int32, (SG_CHUNK, 4 * SG_CHUNK), 0)
    col = lax.broadcasted_iota(jnp.int32, (SG_CHUNK, 4 * SG_CHUNK), 1) % SG_CHUNK
    sg_w = jnp.where(row >= col, sg_w_ref[...], 0.0).astype(BF16)
    for c in range(TS // SG_CHUNK):
        _sgu_chunk(c, za_sc, sg_lng_ref[...], sg_w, sg_bias_ref[...], ya_ref)


def _layer_spec(arr, layer, **kw):
    tail = arr.shape[1:]
    return pl.BlockSpec((None,) + tail, lambda *_: (layer,) + (0,) * len(tail), **kw)


_FRONT_PARAMS = ("g_pre_mix", "w_in", "sg_ln_g", "sg_w", "sg_bias", "cv_w", "cv_b", "cv_ln_g", "cv_ln_b",
                 "cv_pw", "cv_pw_b", "pool_w", "pool_scale")


def _front(layer, x3, prm):
    b, s, _ = x3.shape
    tile = lambda width: pl.BlockSpec((1, TS, width), lambda bi, si: (bi, si, 0))
    pool_rows = POOL_PAD + POOL_HALO + TS
    weights = [prm[k] for k in _FRONT_PARAMS]
    return pl.pallas_call(
        _front_kernel,
        grid=(b, s // TS),
        in_specs=[tile(D_MODEL)] + [_layer_spec(w, layer) for w in weights],
        out_specs=[tile(GW), tile(GW), tile(GW), tile(GW), tile(KV_COLS)],
        out_shape=[
            jax.ShapeDtypeStruct((b, s, GW), BF16),
            jax.ShapeDtypeStruct((b, s, GW), BF16),
            jax.ShapeDtypeStruct((b, s, GW), BF16),
            jax.ShapeDtypeStruct((b, s, GW), F32),
            jax.ShapeDtypeStruct((b, s, KV_COLS), F32),
        ],
        scratch_shapes=[pltpu.VMEM((SUBLANES, CONV_HALO + TS, GW), F32)]
        + [pltpu.VMEM((pool_rows, GW), F32)] * 4
        + [pltpu.VMEM((TS, D_MODEL), BF16),
           pltpu.VMEM((TS, 2 * GW), F32)],
        compiler_params=_cparams(("parallel", "arbitrary")),
        name="front",
    )(x3, *weights)


def _compress_kernel(x_ref, pe_ref, w1_ref, w2_ref, o_ref):
    m = CMP_TB * N_CMP_PAD
    p0 = jnp.zeros((m, 128), F32)
    p1 = jnp.zeros((m, 128), F32)
    for r in range(CMP_STRIDE):
        xr = jnp.concatenate([x_ref[bi, pl.ds(r, N_CMP_PAD, stride=CMP_STRIDE), :] for bi in range(CMP_TB)],
                             axis=0)
        p0 = p0 + _dot((xr + pe_ref[r:r + 1, :]).astype(BF16), w1_ref[r])
        p1 = p1 + _dot((xr + pe_ref[CMP_STRIDE + r:CMP_STRIDE + r + 1, :]).astype(BF16),
                       w1_ref[CMP_STRIDE + r])
    pre = p0 + pltpu.roll(p1, m - 1, axis=0)
    hid = pre * jax.nn.sigmoid(pre)
    out = _dot(hid.astype(BF16), w2_ref[...])
    rowid = lax.broadcasted_iota(jnp.int32, (m, 128), 0) % N_CMP_PAD
    out = jnp.where(rowid < N_CMP_PAD - 1, out, 0.0)
    o_ref[...] = out.reshape(CMP_TB, N_CMP_PAD, 128)


def _compress(layer, zkv, prm):
    b, s, _ = zkv.shape
    weights = [prm[k] for k in ("cmp_pe", "cmp_w1", "cmp_w2")]
    return pl.pallas_call(
        _compress_kernel,
        grid=(b // CMP_TB,),
        in_specs=[pl.BlockSpec((CMP_TB, s, 128), lambda i: (i, 0, 0))] + [_layer_spec(w, layer) for w in weights],
        out_specs=pl.BlockSpec((CMP_TB, N_CMP_PAD, 128), lambda i: (i, 0, 0)),
        out_shape=jax.ShapeDtypeStruct((b, N_CMP_PAD, 128), F32),
        compiler_params=_cparams(("parallel",)),
        name="nsa_compress",
    )(zkv, *weights)


def _nsa_kernel(q_ref, kv_ref, kcv_ref, o_ref, ks_aug, kts, vs_aug, kw_aug, vw_aug, gsel):
    i = pl.program_id(1)
    seq = kv_ref.shape[1]
    nq = Q_BLOCK

    @pl.when(i == 0)
    def _():
        eye = jnp.where(lax.broadcasted_iota(jnp.int32, (AUG, AUG), 0)
                        == lax.broadcasted_iota(jnp.int32, (AUG, AUG), 1), 1.0, 0.0).astype(BF16)
        lane_p = lax.broadcasted_iota(jnp.int32, (WIN, AUG), 1)
        kw_aug[0:WIN, :] = jnp.where(lane_p == PAD_LANE, 1.0, 0.0).astype(BF16)
        vw_aug[0:WIN, :] = jnp.zeros((WIN, 2 * AUG), BF16)
        gsel[...] = jnp.where(lax.broadcasted_iota(jnp.int32, gsel.shape, 0)
                              == lax.broadcasted_iota(jnp.int32, gsel.shape, 1) // AUG, 1.0, 0.0).astype(BF16)
        lane = lax.broadcasted_iota(jnp.int32, (256, AUG), 1)
        ones = jnp.ones((256, AUG), BF16)
        for r in range(0, seq, 256):
            blk = kv_ref[0, r:r + 256, :]
            t_s = blk[:, 128:256]
            t_w = blk[:, 256:384]
            key_blk = (r + lax.broadcasted_iota(jnp.int32, (256, AUG), 0)) // SLC_BLOCK
            onehot = jnp.where(lane - HEAD_DIM == key_blk, 1.0, 0.0)
            ks_aug[r:r + 256, :] = jnp.where(lane < HEAD_DIM, t_s, onehot).astype(BF16)
            v_s = jnp.where(lane < HEAD_DIM, pltpu.roll(t_s, HEAD_DIM, axis=1), 0.0).astype(BF16)
            vs_aug[r:r + 256, :] = jnp.concatenate([v_s, ones], axis=1)
            kw_aug[WIN + r:WIN + r + 256, :] = jnp.where(lane < HEAD_DIM, t_w, 0.0).astype(BF16)
            v_w = jnp.where(lane < HEAD_DIM, pltpu.roll(t_w, HEAD_DIM, axis=1), 0.0).astype(BF16)
            vw_aug[WIN + r:WIN + r + 256, :] = jnp.concatenate([v_w, ones], axis=1)
        for c in range(0, seq, SLC_TK):
            kts[:, c:c + SLC_TK] = _nt(eye, ks_aug[c:c + SLC_TK, :]).astype(BF16)

    per = SLC_TK // nq
    n_dyn = (i + per - 1) // per
    for n_kt in range(seq // SLC_TK + 1):
        @pl.when(n_dyn == n_kt)
        def _(n_kt=n_kt):
            _nsa_tile(n_kt, i, q_ref, kv_ref, kcv_ref, o_ref, ks_aug, kts, vs_aug, kw_aug, vw_aug, gsel)


def _nsa_tile(n_kt, i, q_ref, kv_ref, kcv_ref, o_ref, ks_aug, kts, vs_aug, kw_aug, vw_aug, gsel):
    nq = Q_BLOCK
    hq = N_HEADS * nq
    eye = jnp.where(lax.broadcasted_iota(jnp.int32, (AUG, AUG), 0)
                    == lax.broadcasted_iota(jnp.int32, (AUG, AUG), 1), 1.0, 0.0).astype(BF16)
    q0 = pl.multiple_of(i * nq, nq)
    q = q_ref[0] * (HEAD_DIM ** -0.5 * LOG2E)
    lane_q = lax.broadcasted_iota(jnp.int32, (nq, AUG), 1)

    def stack_heads(extra):
        tiles = []
        for h in range(N_HEADS):
            t = q[:, AUG * (h // 2):AUG * (h // 2 + 1)]
            if h % 2:
                t = pltpu.roll(t, HEAD_DIM, axis=1)
            tiles.append(jnp.where(lane_q < HEAD_DIM, t, extra))
        return jnp.concatenate(tiles, axis=0).astype(BF16)

    pad_bias = jnp.where(lane_q == PAD_LANE, NEG, 0.0)
    qw = stack_heads(pad_bias)
    trow = q0 + lax.broadcasted_iota(jnp.int32, (nq, 1), 0)
    row_l = lax.broadcasted_iota(jnp.int32, (nq, nq), 0)
    col_l = lax.broadcasted_iota(jnp.int32, (nq, nq), 1)
    tri_le = (col_l <= row_l)[None]
    tri_gt = (col_l > row_l)[None]

    def window_branch():
        nk = WIN + nq
        sw = _nt(qw, kw_aug[pl.ds(q0, nk), :]).reshape(N_HEADS, nq, nk)
        sw = jnp.concatenate([jnp.where(tri_gt, sw[:, :, :nq], NEG), sw[:, :, nq:WIN],
                              jnp.where(tri_le, sw[:, :, WIN:], NEG)], axis=-1)
        pw = jnp.exp2(sw - jnp.max(sw, axis=-1, keepdims=True)).astype(BF16)
        return _dot(pw.reshape(hq, nk), vw_aug[pl.ds(q0, nk), :])

    window_first = not (1 <= n_kt * SLC_TK <= WINDOW_LATE_MAX_KEYS)
    if window_first:
        o_win = window_branch()

    kcv = kcv_ref[0]
    lane_c = lax.broadcasted_iota(jnp.int32, (N_CMP_PAD, AUG), 1)
    kc = jnp.where(lane_c < HEAD_DIM, kcv, 0.0).astype(BF16)
    vc = jnp.where(lane_c < HEAD_DIM, pltpu.roll(kcv, HEAD_DIM, axis=1), 0.0).astype(BF16)
    sd = _nt(qw, ks_aug[pl.ds(q0, nq), :]).reshape(N_HEADS, nq, nq)
    sd = jnp.where(tri_le, sd, NEG).reshape(hq, nq)
    md = jnp.max(sd, axis=-1, keepdims=True)

    s = _nt(qw, kc).reshape(N_HEADS, nq, N_CMP_PAD)
    cidx = lax.broadcasted_iota(jnp.int32, (nq, N_CMP_PAD), 1)
    cmask = (cidx * CMP_STRIDE + (CMP_BLOCK - 1) <= trow) & (cidx < N_CMP_PAD - 1)
    sm = jnp.where(cmask[None], s, NEG)
    e = jnp.exp2(sm - jnp.max(sm, axis=-1, keepdims=True))
    p = e / jnp.sum(e, axis=-1, keepdims=True)
    p = jnp.where((trow >= CMP_BLOCK - 1)[None], p, 0.0)
    o_cmp = _dot(p.reshape(hq, N_CMP_PAD).astype(BF16), vc)

    if not window_first:
        o_win = window_branch()

    psum = jnp.sum(p, axis=0)
    p_hi = psum.astype(BF16)
    p_lo = (psum - p_hi.astype(F32)).astype(BF16)
    jrow = lax.broadcasted_iota(jnp.int32, (N_SLC, N_CMP_PAD), 0)
    ccol = lax.broadcasted_iota(jnp.int32, (N_SLC, N_CMP_PAD), 1)
    ov = ((ccol * CMP_STRIDE <= jrow * SLC_BLOCK + (SLC_BLOCK - 1))
          & (ccol * CMP_STRIDE + (CMP_BLOCK - 1) >= jrow * SLC_BLOCK)
          & (ccol < N_CMP_PAD - 1))
    ov = jnp.where(ov, 1.0, 0.0).astype(BF16)
    imp = _nt(ov, p_hi) + _nt(ov, p_lo)
    jj = lax.broadcasted_iota(jnp.int32, (N_SLC, nq), 0)
    tt = q0 + lax.broadcasted_iota(jnp.int32, (N_SLC, nq), 1)
    cur = tt // SLC_BLOCK
    valid = jj * SLC_BLOCK <= tt
    forced = (jj == 0) | (jj == cur) | (jj == cur - 1)
    score = jnp.where(valid, imp + jnp.where(forced, FORCE_BONUS, 0.0), NEG)
    rank = jnp.zeros((N_SLC, nq), F32)
    for j2 in range(N_SLC):
        other = score[j2:j2 + 1, :]
        beats = (other > score) | ((other == score) & (jj > j2))
        rank = rank + jnp.where(beats, 1.0, 0.0)
    sel_t = jnp.where((rank < SLC_TOPK) & (score > NEG / 2), 1.0, 0.0)
    bias_t = jnp.where((sel_t > 0.5) & (jj < 2 * i), 0.0, NEG)
    bias_t = jnp.concatenate([jnp.zeros((HEAD_DIM, nq), F32), bias_t,
                              jnp.zeros((AUG - HEAD_DIM - N_SLC, nq), F32)], axis=0).astype(BF16)
    blk_bias = _nt(eye, bias_t)

    if n_kt:
        n_main = n_kt * SLC_TK
        qs = stack_heads(blk_bias + pad_bias)
        k_main = kts[:, 0:n_main]
        v_main = vs_aug[0:n_main, :]
        v_diag = vs_aug[pl.ds(q0, nq), :]
        n_grp = SLC_ROW_GROUPS if n_main >= SLC_SPLIT_KEYS else 1
        rows = hq // n_grp
        by_rows = lambda a, b: jnp.concatenate([_dot(a[r0:r0 + rows], b) for r0 in range(0, hq, rows)], axis=0)
        s_main = by_rows(qs, k_main)
        m = jnp.maximum(md, jnp.max(s_main, axis=-1, keepdims=True))
        p_main = jnp.exp2(s_main - m).astype(BF16)
        p_diag = jnp.exp2(sd - m).astype(BF16)
        o_slc = _dot(p_diag, v_diag) + by_rows(p_main, v_main)
    else:
        o_slc = _dot(jnp.exp2(sd - md).astype(BF16), vs_aug[pl.ds(q0, nq), :])

    g = jax.nn.sigmoid(kv_ref[0, pl.ds(q0, nq), 384:384 + AUG])
    g_hi = g.astype(BF16)
    g_lo = (g - g_hi.astype(F32)).astype(BF16)
    g_rep = _dot(g_hi, gsel[...]) + _dot(g_lo, gsel[...])
    gate = lambda br: jnp.concatenate(
        [g_rep[:, (3 * h + br) * AUG:(3 * h + br + 1) * AUG] for h in range(N_HEADS)], axis=0)
    o = (gate(0) * o_cmp
         + (gate(1) / o_slc[:, AUG:]) * o_slc[:, :AUG]
         + (gate(2) / o_win[:, AUG:]) * o_win[:, :AUG])
    halves = [jnp.where(lane_q < HEAD_DIM, o[2 * a * nq:(2 * a + 1) * nq, :],
                        pltpu.roll(o[(2 * a + 1) * nq:(2 * a + 2) * nq, :], HEAD_DIM, axis=1))
              for a in range(N_HEADS // 2)]
    o_ref[0] = jnp.concatenate(halves, axis=1).astype(BF16)


def _nsa(zq, zkv, kcv):
    b, s, _ = zq.shape
    return pl.pallas_call(
        _nsa_kernel,
        grid=(b, s // Q_BLOCK),
        in_specs=[
            pl.BlockSpec((1, Q_BLOCK, GW), lambda bi, qi: (bi, qi, 0)),
            pl.BlockSpec((1, s, KV_COLS), lambda bi, qi: (bi, 0, 0)),
            pl.BlockSpec((1, N_CMP_PAD, 128), lambda bi, qi: (bi, 0, 0)),
        ],
        out_specs=pl.BlockSpec((1, Q_BLOCK, GW), lambda bi, qi: (bi, qi, 0)),
        out_shape=jax.ShapeDtypeStruct((b, s, GW), BF16),
        scratch_shapes=[
            pltpu.VMEM((s, AUG), BF16),
            pltpu.VMEM((AUG, s), BF16),
            pltpu.VMEM((s, 2 * AUG), BF16),
            pltpu.VMEM((WIN + s, AUG), BF16),
            pltpu.VMEM((WIN + s, 2 * AUG), BF16),
            pltpu.VMEM((AUG, 3 * N_HEADS * AUG), BF16),
        ],
        compiler_params=_cparams(("parallel", "arbitrary")),
        name="nsa_attn",
    )(zq, zkv, kcv)


def _back_kernel(ya_ref, yb_ref, yc_ref, yd_ref, x_ref, wo_ref, gmix_ref, gpre_ref, wgu_ref, wd_ref,
                 gpost_ref, o_ref):
    xs, hs = [], []
    for r0 in range(0, TM, TM // 2):
        rows = slice(r0, r0 + TM // 2)
        y = jnp.concatenate([ya_ref[rows, :], yb_ref[rows, :], yc_ref[rows, :], yd_ref[rows, :]], axis=1)
        x_half = x_ref[rows, :] + _rms(_dot(y, wo_ref[...]), gmix_ref[...])
        xs.append(x_half)
        hs.append(_rms(x_half, gpre_ref[...]).astype(BF16))
    x = jnp.concatenate(xs, axis=0)
    h = jnp.concatenate(hs, axis=0)
    f = jnp.zeros((TM, D_MODEL), F32)
    c0 = 0
    for width in FFN_CHUNKS:
        gate = _dot(h, wgu_ref[:, c0:c0 + width])
        up = _dot(h, wgu_ref[:, FFN_HIDDEN + c0:FFN_HIDDEN + c0 + width])
        act = ((gate * jax.nn.sigmoid(gate)) * up).astype(BF16)
        f = f + _dot(act, wd_ref[c0:c0 + width, :])
        c0 += width
    o_ref[...] = x + _rms(f, gpost_ref[...])


_BACK_PARAMS = ("w_out", "g_post_mix", "g_pre_ffn", "ffn_w_gu", "ffn_w_down", "g_post_ffn")


def _back(layer, ya, yb, yc, yd, x2, prm):
    n = x2.shape[0]
    yspec = pl.BlockSpec((TM, GW), lambda i: (i, 0))
    weights = [prm[k] for k in _BACK_PARAMS]
    return pl.pallas_call(
        _back_kernel,
        grid=(n // TM,),
        in_specs=[yspec, yspec, yspec, yspec, pl.BlockSpec((TM, D_MODEL), lambda i: (i, 0))]
        + [_layer_spec(w, layer, pipeline_mode=pl.Buffered(1)) for w in weights],
        out_specs=pl.BlockSpec((TM, D_MODEL), lambda i: (i, 0)),
        out_shape=jax.ShapeDtypeStruct((n, D_MODEL), F32),
        compiler_params=_cparams(("parallel",)),
        name="back",
    )(ya, yb, yc, yd, x2, *weights)


def _interleave(wk, wv):
    n_layers = wk.shape[0]
    z = jnp.zeros((n_layers, CMP_BLOCK, HEAD_DIM, HEAD_DIM), wk.dtype)
    wk4 = wk.reshape(n_layers, CMP_BLOCK, HEAD_DIM, HEAD_DIM)
    wv4 = wv.reshape(n_layers, CMP_BLOCK, HEAD_DIM, HEAD_DIM)
    return jnp.concatenate([jnp.concatenate([wk4, z], axis=3), jnp.concatenate([z, wv4], axis=3)], axis=2)


def _block_diag(mats):
    n = len(mats)
    rows = []
    for a, m in enumerate(mats):
        rows.append(jnp.concatenate([m if a == c else jnp.zeros_like(m) for c in range(n)], axis=2))
    return jnp.concatenate(rows, axis=1)


def _prepare(p):
    n_layers = p["w_in"].shape[0]
    row = lambda v: v.reshape(n_layers, 1, -1)
    w_in = p["w_in"]
    w_in = jnp.concatenate(
        [w_in[:, :, :1280], w_in[:, :, 1676:1932], w_in[:, :, 1280:1676],
         jnp.zeros((n_layers, D_MODEL, Z_COLS - 1932), w_in.dtype)], axis=2)
    return {
        "g_pre_mix": row(p["g_pre_mix"]), "g_post_mix": row(p["g_post_mix"]),
        "g_pre_ffn": row(p["g_pre_ffn"]), "g_post_ffn": row(p["g_post_ffn"]),
        "w_in": w_in.astype(BF16), "w_out": p["w_out"].astype(BF16),
        "ffn_w_gu": p["ffn_w_gu"].astype(BF16), "ffn_w_down": p["ffn_w_down"].astype(BF16),
        "sg_ln_g": row(p["sg_ln_g"]),
        "sg_w": jnp.transpose(p["sg_w"], (0, 2, 1, 3)).reshape(n_layers, SG_CHUNK, N_HEADS * SG_CHUNK),
        "sg_bias": jnp.repeat(jnp.swapaxes(p["sg_b"], 1, 2), HEAD_DIM, axis=2),
        "cv_w": p["cv_w"], "cv_b": row(p["cv_b"]), "cv_ln_g": row(p["cv_ln_g"]), "cv_ln_b": row(p["cv_ln_b"]),
        "cv_pw": p["cv_pw"].astype(BF16), "cv_pw_b": row(p["cv_pw_b"]),
        "pool_w": _block_diag([p["pool_w"][:, gi] for gi in range(len(POOL_WINDOWS))]).astype(BF16),
        "pool_scale": row(p["pool_scale"]),
        "cmp_pe": jnp.concatenate([p["cmp_pos_k"], p["cmp_pos_v"]], axis=2),
        "cmp_w1": _interleave(p["cmp_w1_k"], p["cmp_w1_v"]).astype(BF16),
        "cmp_w2": _block_diag([p["cmp_w2_k"], p["cmp_w2_v"]]).astype(BF16),
    }


def _layer(layer, x2, batch, seq, prm):
    n = x2.shape[0]
    y_a, y_b, y_d, zq, zkv = _front(layer, x2.reshape(batch, seq, D_MODEL), prm)
    y_c = _nsa(zq, zkv, _compress(layer, zkv, prm))
    flat = lambda y: y.reshape(n, GW)
    return _back(layer, flat(y_a), flat(y_b), flat(y_c), flat(y_d), x2, prm)


_PARAM_NAMES = ("g_pre_mix", "g_post_mix", "g_pre_ffn", "g_post_ffn", "w_in", "sg_ln_g", "sg_w", "sg_b",
                "cv_w", "cv_b", "cv_ln_g", "cv_ln_b", "cv_pw", "cv_pw_b", "cmp_pos_k", "cmp_pos_v",
                "cmp_w1_k", "cmp_w2_k", "cmp_w1_v", "cmp_w2_v", "pool_w", "pool_scale", "w_out",
                "ffn_w_gu", "ffn_w_down")


def kernel(x, g_pre_mix, g_post_mix, g_pre_ffn, g_post_ffn, w_in, sg_ln_g, sg_w, sg_b, cv_w, cv_b, cv_ln_g, cv_ln_b, cv_pw, cv_pw_b, cmp_pos_k, cmp_pos_v, cmp_w1_k, cmp_w2_k, cmp_w1_v, cmp_w2_v, pool_w, pool_scale, w_out, ffn_w_gu, ffn_w_down):
    params = dict(zip(_PARAM_NAMES, (g_pre_mix, g_post_mix, g_pre_ffn, g_post_ffn, w_in, sg_ln_g, sg_w,
                                     sg_b, cv_w, cv_b, cv_ln_g, cv_ln_b, cv_pw, cv_pw_b, cmp_pos_k,
                                     cmp_pos_v, cmp_w1_k, cmp_w2_k, cmp_w1_v, cmp_w2_v, pool_w,
                                     pool_scale, w_out, ffn_w_gu, ffn_w_down)))
    batch, seq, _ = x.shape
    prm = _prepare(params)
    x2 = x.reshape(batch * seq, D_MODEL)
    for layer in range(g_pre_mix.shape[0]):
        x2 = _layer(layer, x2, batch, seq, prm)
    return x2.reshape(batch, seq, D_MODEL)
```
